```python
import jax, jax.numpy as jnp
from jax import lax
import numpy as np

D_MODEL = 1024
BATCH = 4
SEQ = 8192
DEPTH = 1

FOURIER_GROUPS = 4
FOURIER_WIDTH = D_MODEL // 2
FOURIER_GROUP_DIM = FOURIER_WIDTH // FOURIER_GROUPS
MLA_HEADS = 8
QK_NOPE_DIM = 64
QK_ROPE_DIM = 32
V_HEAD_DIM = (D_MODEL // 2) // MLA_HEADS
Q_LORA_RANK = 384
KV_LORA_RANK = 256
Q_BLOCK = 128
ROPE_THETA = 10000.0
D_FF = 2816
NORM_EPS = 1e-6
N_ADA = 9
W_IN_COLS = FOURIER_WIDTH + Q_LORA_RANK + KV_LORA_RANK + QK_ROPE_DIM + 2 * D_MODEL
SPLIT_1 = FOURIER_WIDTH
SPLIT_2 = SPLIT_1 + Q_LORA_RANK
SPLIT_3 = SPLIT_2 + KV_LORA_RANK
SPLIT_4 = SPLIT_3 + QK_ROPE_DIM

kernel_name = "hybrid_fourier_mla_macaron_adaln_encoder"


def rms_norm(x, g):
    x32 = x.astype(jnp.float32)
    y = x32 * lax.rsqrt(jnp.mean(x32 * x32, axis=-1, keepdims=True) + NORM_EPS)
    return (y * g.astype(jnp.float32)).astype(x.dtype)


def modulate(h, shift, scale):
    return h * (1 + scale[:, None, :]) + shift[:, None, :]


def swiglu(h, w_gate, w_up, w_down):
    return (jax.nn.silu(h @ w_gate) * (h @ w_up)) @ w_down


def rope_tables(positions):
    half = QK_ROPE_DIM // 2
    inv_freq = 1.0 / (ROPE_THETA ** (jnp.arange(half, dtype=jnp.float32) * 2.0 / QK_ROPE_DIM))
    ang = positions.astype(jnp.float32)[..., None] * inv_freq
    return jnp.cos(ang), jnp.sin(ang)


def apply_rope(x, cos, sin):
    x32 = x.astype(jnp.float32)
    x1, x2 = jnp.split(x32, 2, axis=-1)
    out = jnp.concatenate([x1 * cos - x2 * sin, x2 * cos + x1 * sin], axis=-1)
    return out.astype(x.dtype)


def fourier_mix(u):
    b, s, _ = u.shape
    ug = u.reshape(b, s, FOURIER_GROUPS, FOURIER_GROUP_DIM).astype(jnp.float32)
    f = jnp.fft.fft2(ug, axes=(1, 3), norm="ortho").real
    return f.reshape(b, s, FOURIER_WIDTH).astype(u.dtype)


def mla_attention(q_lat, kv_lat, k_rope, positions, q_norm, w_q_up, kv_norm, w_kv_up):
    b, s, _ = q_lat.shape
    q = (rms_norm(q_lat, q_norm) @ w_q_up).reshape(b, s, MLA_HEADS, QK_NOPE_DIM + QK_ROPE_DIM)
    kv = (rms_norm(kv_lat, kv_norm) @ w_kv_up).reshape(b, s, MLA_HEADS, QK_NOPE_DIM + V_HEAD_DIM)
    q_nope, q_rot = q[..., :QK_NOPE_DIM], q[..., QK_NOPE_DIM:]
    k_nope, v = kv[..., :QK_NOPE_DIM], kv[..., QK_NOPE_DIM:]
    cos, sin = rope_tables(positions)
    q_rot = apply_rope(q_rot, cos[:, :, None, :], sin[:, :, None, :])
    k_rot = apply_rope(k_rope, cos, sin)
    sm_scale = (QK_NOPE_DIM + QK_ROPE_DIM) ** -0.5
    q_nope = q_nope * sm_scale
    q_rot = q_rot * sm_scale
    n_blk = s // Q_BLOCK
    qn_b = q_nope.reshape(b, n_blk, Q_BLOCK, MLA_HEADS, QK_NOPE_DIM).transpose(1, 0, 2, 3, 4)
    qr_b = q_rot.reshape(b, n_blk, Q_BLOCK, MLA_HEADS, QK_ROPE_DIM).transpose(1, 0, 2, 3, 4)

    def attend(blk):
        qn_i, qr_i = blk
        scores = (jnp.einsum('bqhd,bkhd->bhqk', qn_i, k_nope)
                  + jnp.einsum('bqhr,bkr->bhqk', qr_i, k_rot))
        p = jax.nn.softmax(scores.astype(jnp.float32), axis=-1).astype(v.dtype)
        return jnp.einsum('bhqk,bkhd->bqhd', p, v)

    o = lax.map(attend, (qn_b, qr_b))
    return o.transpose(1, 0, 2, 3, 4).reshape(b, s, MLA_HEADS * V_HEAD_DIM)


def setup_inputs(seed: int = 0) -> dict:
    key = jax.random.key(seed)
    ks = jax.random.split(key, 24)
    f32 = jnp.float32

    def w(k, shape, fan_in, gain=1.0):
        return (jax.random.normal(k, shape, f32) * (gain * fan_in ** -0.5)).astype(f32)

    def gain(k, shape):
        return 1.0 + 0.02 * jax.random.normal(k, shape, f32)

    L, D = DEPTH, D_MODEL
    x = jax.random.normal(ks[0], (BATCH, SEQ, D), f32)
    c = jax.random.normal(ks[1], (BATCH, D), f32)
    offs = jax.random.randint(ks[2], (BATCH, 1), 0, 1024, dtype=jnp.int32)
    positions = (offs + jnp.arange(SEQ, dtype=jnp.int32)[None, :]).astype(jnp.int32)
    return {
        "x": x,
        "c": c,
        "positions": positions,
        "ada_w": w(ks[3], (L, D, N_ADA * D), D, 0.2),
        "ada_b": 0.02 * jax.random.normal(ks[4], (L, N_ADA * D), f32),
        "ffn1_norm": gain(ks[5], (L, D)),
        "ffn1_w_gate": w(ks[6], (L, D, D_FF), D),
        "ffn1_w_up": w(ks[7], (L, D, D_FF), D),
        "ffn1_w_down": w(ks[8], (L, D_FF, D), D_FF),
        "mix_norm": gain(ks[9], (L, D)),
        "w_in": w(ks[10], (L, D, W_IN_COLS), D),
        "q_norm": gain(ks[11], (L, Q_LORA_RANK)),
        "w_q_up": w(ks[12], (L, Q_LORA_RANK, MLA_HEADS * (QK_NOPE_DIM + QK_ROPE_DIM)), Q_LORA_RANK),
        "kv_norm": gain(ks[13], (L, KV_LORA_RANK)),
        "w_kv_up": w(ks[14], (L, KV_LORA_RANK, MLA_HEADS * (QK_NOPE_DIM + V_HEAD_DIM)), KV_LORA_RANK),
        "w_fourier_out": w(ks[15], (L, FOURIER_WIDTH, D), FOURIER_WIDTH),
        "w_mla_out": w(ks[16], (L, MLA_HEADS * V_HEAD_DIM, D), MLA_HEADS * V_HEAD_DIM),
        "w_out": w(ks[17], (L, D, D), D),
        "ffn2_norm": gain(ks[18], (L, D)),
        "ffn2_w_gate": w(ks[19], (L, D, D_FF), D),
        "ffn2_w_up": w(ks[20], (L, D, D_FF), D),
        "ffn2_w_down": w(ks[21], (L, D_FF, D), D_FF),
        "final_norm": gain(ks[22], (D,)),
    }


def reference(x, c, positions, ada_w, ada_b, ffn1_norm, ffn1_w_gate, ffn1_w_up, ffn1_w_down,
              mix_norm, w_in, q_norm, w_q_up, kv_norm, w_kv_up, w_fourier_out, w_mla_out, w_out,
              ffn2_norm, ffn2_w_gate, ffn2_w_up, ffn2_w_down, final_norm):
    c_act = jax.nn.silu(c)
    for l in range(DEPTH):
        mod = c_act @ ada_w[l] + ada_b[l]
        (sh1, sc1, g1, sh2, sc2, g2, sh3, sc3, g3) = jnp.split(mod, N_ADA, axis=-1)

        h = modulate(rms_norm(x, ffn1_norm[l]), sh1, sc1)
        x = x + 0.5 * g1[:, None, :] * swiglu(h, ffn1_w_gate[l], ffn1_w_up[l], ffn1_w_down[l])

        h = modulate(rms_norm(x, mix_norm[l]), sh2, sc2)
        z = h @ w_in[l]
        u_f, q_lat, kv_lat, k_rope, gate_logits = jnp.split(
            z, [SPLIT_1, SPLIT_2, SPLIT_3, SPLIT_4], axis=-1)
        y_a = fourier_mix(u_f) @ w_fourier_out[l]
        y_b = mla_attention(q_lat, kv_lat, k_rope, positions, q_norm[l], w_q_up[l],
                            kv_norm[l], w_kv_up[l]) @ w_mla_out[l]
        gate_a, gate_b = jnp.split(jax.nn.sigmoid(gate_logits), 2, axis=-1)
        y = (gate_a * y_a + gate_b * y_b) @ w_out[l]
        x = x + g2[:, None, :] * y

        h = modulate(rms_norm(x, ffn2_norm[l]), sh3, sc3)
        x = x + 0.5 * g3[:, None, :] * swiglu(h, ffn2_w_gate[l], ffn2_w_up[l], ffn2_w_down[l])
    return rms_norm(x, final_norm)
```

```python
import functools

import numpy as np
import jax
import jax.numpy as jnp
from jax import lax
from jax.experimental import pallas as pl
from jax.experimental.pallas import tpu as pltpu

F32 = jnp.float32
BF16 = jnp.bfloat16

FOURIER_GROUPS = 4
FOURIER_GROUP_DIM = 128
FOURIER_WIDTH = FOURIER_GROUPS * FOURIER_GROUP_DIM
MLA_HEADS = 8
QK_NOPE_DIM = 64
QK_ROPE_DIM = 32
QK_DIM = QK_NOPE_DIM + QK_ROPE_DIM
V_HEAD_DIM = 64
Q_LORA_RANK = 384
KV_LORA_RANK = 256
ROPE_THETA = 10000.0
NORM_EPS = 1e-6
N_ADA = 9

HEAD_PAD = 128
ROPE_HALF = QK_ROPE_DIM // 2
FFT_N1 = 128
ROW_TILE = 512
KV_TILE = 512
Q_TILE = 512
FFT1_COLS = 8
FFT3_ROWS = 8
VMEM_LIMIT = 56 * 1024 * 1024


def _params(n_axes):
    return pltpu.CompilerParams(
        dimension_semantics=("parallel",) * n_axes, vmem_limit_bytes=VMEM_LIMIT)


def _const_spec(shape):
    nd = len(shape)
    return pl.BlockSpec(shape, lambda *_: (0,) * nd, pipeline_mode=pl.Buffered(1))


def _rms(x, g):
    ms = jnp.mean(x * x, axis=-1, keepdims=True)
    return x * lax.rsqrt(ms + NORM_EPS) * g


def _norm_mod(x, g, shift, scale):
    return _rms(x, g) * (1.0 + scale) + shift


def _dot(a, b):
    return jnp.dot(a, b, preferred_element_type=F32)


def _dot_nt(a, b):
    return lax.dot_general(a, b, (((1,), (1,)), ((), ())), preferred_element_type=F32)


def _dot_tn(a, b):
    return lax.dot_general(a, b, (((0,), (0,)), ((), ())), preferred_element_type=F32)


def _ada_kernel(c_ref, w_ref, b_ref, o_ref):
    c = c_ref[...]
    ca = c * jax.nn.sigmoid(c)
    o_ref[...] = jnp.dot(ca, w_ref[...], preferred_element_type=F32,
                         precision=lax.Precision.HIGHEST) + b_ref[...]


def _ada(c, ada_w, ada_b):
    b, d = c.shape
    rows = 8
    c_pad = jnp.zeros((rows, d), F32).at[:b].set(c)
    out = pl.pallas_call(
        _ada_kernel,
        grid=(N_ADA,),
        in_specs=[pl.BlockSpec((rows, d), lambda j: (0, 0)),
                  pl.BlockSpec((d, d), lambda j: (0, j)),
                  pl.BlockSpec((1, d), lambda j: (0, j))],
        out_specs=pl.BlockSpec((rows, d), lambda j: (0, j)),
        out_shape=jax.ShapeDtypeStruct((rows, N_ADA * d), F32),
        compiler_params=_params(1),
        name="ada",
    )(c_pad, ada_w, ada_b.reshape(1, -1))
    return out[:b].reshape(b, N_ADA, d)


def _ffn_kernel(x_ref, mod_ref, g_ref, wg_ref, wu_ref, wd_ref, fn_ref, o_ref, *, sub, final):
    x = x_ref[...]
    shift = mod_ref[3 * sub:3 * sub + 1, :]
    scale = mod_ref[3 * sub + 1:3 * sub + 2, :]
    gate = mod_ref[3 * sub + 2:3 * sub + 3, :]
    h = _norm_mod(x, g_ref[...], shift, scale).astype(BF16)
    gg = _dot(h, wg_ref[...])
    uu = _dot(h, wu_ref[...])
    a = (gg * jax.nn.sigmoid(gg) * uu).astype(BF16)
    y = x + (0.5 * gate) * _dot(a, wd_ref[...])
    if final:
        y = _rms(y, fn_ref[...])
    o_ref[...] = y


def _ffn(x2d, mod, norm_g, wg, wu, wd, final_g, *, sub, final, seq):
    n, d = x2d.shape
    dff = wg.shape[1]
    per_batch = seq // ROW_TILE
    return pl.pallas_call(
        functools.partial(_ffn_kernel, sub=sub, final=final),
        grid=(n // ROW_TILE,),
        in_specs=[pl.BlockSpec((ROW_TILE, d), lambda i: (i, 0)),
                  pl.BlockSpec((None, N_ADA, d), lambda i: (i // per_batch, 0, 0)),
                  _const_spec((1, d)),
                  _const_spec((d, dff)), _const_spec((d, dff)), _const_spec((dff, d)),
                  _const_spec((1, d))],
        out_specs=pl.BlockSpec((ROW_TILE, d), lambda i: (i, 0)),
        out_shape=jax.ShapeDtypeStruct((n, d), F32),
        compiler_params=_params(1),
        name="ffn%d" % sub,
    )(x2d, mod, norm_g.reshape(1, d), wg, wu, wd, final_g.reshape(1, d))


def _mix_in_kernel(x_ref, mod_ref, g_ref, pos_ref, invf_ref, w_in_ref, w_krt_ref, qn_ref, kvn_ref,
                   w_qt_ref, w_knt_ref, w_vt_ref, cs_ref,
                   ab_ref, qt_ref, k_ref, vt_ref, *, q_scale):
    x = x_ref[...]
    h = _norm_mod(x, g_ref[...], mod_ref[3:4, :], mod_ref[4:5, :]).astype(BF16)
    z = _dot(h, w_in_ref[...])
    tile = x.shape[0]

    cs = cs_ref[...]
    for g in range(FOURIER_GROUPS):
        lo = g * FOURIER_GROUP_DIM
        u = z[:, lo:lo + FOURIER_GROUP_DIM].astype(BF16)
        ab = _dot(u, cs)
        ab_ref[:, lo:lo + FOURIER_GROUP_DIM] = ab[:, :FOURIER_GROUP_DIM].astype(BF16)
        ab_ref[:, FOURIER_WIDTH + lo:FOURIER_WIDTH + lo + FOURIER_GROUP_DIM] = (
            ab[:, FOURIER_GROUP_DIM:].astype(BF16))

    ang = invf_ref[...] * pos_ref[...]
    cos_t = jnp.cos(ang)
    sin_t = jnp.sin(ang)

    def rope_t(v):
        v1, v2 = v[:ROPE_HALF], v[ROPE_HALF:]
        return v1 * cos_t - v2 * sin_t, v2 * cos_t + v1 * sin_t

    pad = jnp.zeros((HEAD_PAD - QK_DIM, tile), F32)

    q_lat = z[:, FOURIER_WIDTH:FOURIER_WIDTH + Q_LORA_RANK]
    qn = _rms(q_lat, qn_ref[...]).astype(BF16)
    qt = _dot_nt(w_qt_ref[...], qn)
    for hd in range(MLA_HEADS):
        lo = hd * QK_DIM
        r1, r2 = rope_t(qt[lo + QK_NOPE_DIM:lo + QK_DIM])
        qh = jnp.concatenate([qt[lo:lo + QK_NOPE_DIM], r1, r2], axis=0) * q_scale
        qt_ref[hd * HEAD_PAD:(hd + 1) * HEAD_PAD, :] = (
            jnp.concatenate([qh, pad], axis=0).astype(BF16))

    kv_lat = z[:, FOURIER_WIDTH + Q_LORA_RANK:]
    kvn = _rms(kv_lat, kvn_ref[...]).astype(BF16)
    knt = _dot_nt(w_knt_ref[...], kvn)
    vt = _dot_nt(w_vt_ref[...], kvn)
    kr1, kr2 = rope_t(_dot_nt(w_krt_ref[...], h))
    for hd in range(MLA_HEADS):
        lo = hd * QK_NOPE_DIM
        kt = jnp.concatenate([knt[lo:lo + QK_NOPE_DIM], kr1, kr2, pad], axis=0)
        k_ref[:, hd * HEAD_PAD:(hd + 1) * HEAD_PAD] = kt.T.astype(BF16)
        vt_ref[hd, 0] = vt[hd * V_HEAD_DIM:(hd + 1) * V_HEAD_DIM].astype(BF16)


def _mix_in(x2d, mod, norm_g, pos_rows, inv_freq, w_in_a, w_krt, q_norm, kv_norm, w_qt, w_knt, w_vt,
            cs, *, batch, seq, q_scale):
    n, d = x2d.shape
    per_batch = seq // ROW_TILE
    assert ROW_TILE == KV_TILE
    ispec = [pl.BlockSpec((ROW_TILE, d), lambda i: (i, 0)),
             pl.BlockSpec((None, N_ADA, d), lambda i: (i // per_batch, 0, 0)),
             _const_spec((1, d)),
             pl.BlockSpec((None, 1, ROW_TILE), lambda i: (i, 0, 0)),
             _const_spec(inv_freq.shape),
             _const_spec(w_in_a.shape), _const_spec(w_krt.shape),
             _const_spec((1, Q_LORA_RANK)), _const_spec((1, KV_LORA_RANK)),
             _const_spec(w_qt.shape), _const_spec(w_knt.shape), _const_spec(w_vt.shape),
             _const_spec(cs.shape)]
    hp = MLA_HEADS * HEAD_PAD
    ospec = [pl.BlockSpec((ROW_TILE, 2 * FOURIER_WIDTH), lambda i: (i, 0)),
             pl.BlockSpec((None, hp, ROW_TILE), lambda i: (i // per_batch, 0, i % per_batch)),
             pl.BlockSpec((ROW_TILE, hp), lambda i: (i, 0)),
             pl.BlockSpec((None, MLA_HEADS, 1, V_HEAD_DIM, KV_TILE),
                          lambda i: (i // per_batch, 0, i % per_batch, 0, 0))]
    oshape = [jax.ShapeDtypeStruct((n, 2 * FOURIER_WIDTH), BF16),
              jax.ShapeDtypeStruct((batch, hp, seq), BF16),
              jax.ShapeDtypeStruct((n, hp), BF16),
              jax.ShapeDtypeStruct((batch, MLA_HEADS, seq // KV_TILE, V_HEAD_DIM, KV_TILE), BF16)]
    return pl.pallas_call(
        functools.partial(_mix_in_kernel, q_scale=q_scale),
        grid=(n // ROW_TILE,),
        in_specs=ispec, out_specs=ospec, out_shape=oshape,
        compiler_params=_params(1),
        name="mix_in",
    )(x2d, mod, norm_g.reshape(1, d), pos_rows, inv_freq, w_in_a, w_krt,
      q_norm.reshape(1, -1), kv_norm.reshape(1, -1), w_qt, w_knt, w_vt, cs)


def _fft1_kernel(x_ref, w_ref, tc_ref, ts_ref, o_ref):
    w = w_ref[...]
    wd = 2 * FOURIER_WIDTH
    for j in range(FFT1_COLS):
        a = x_ref[:, j * wd:j * wd + FOURIER_WIDTH]
        b = x_ref[:, j * wd + FOURIER_WIDTH:(j + 1) * wd]
        g = _dot(w, jnp.concatenate([a, b], axis=0))
        gr, gi = g[:FFT_N1], g[FFT_N1:]
        tc = jnp.concatenate([tc_ref[j]] * FOURIER_GROUPS, axis=1)
        ts = jnp.concatenate([ts_ref[j]] * FOURIER_GROUPS, axis=1)
        o_ref[:, j * wd:j * wd + FOURIER_WIDTH] = (gr * tc - gi * ts).astype(BF16)
        o_ref[:, j * wd + FOURIER_WIDTH:(j + 1) * wd] = (gi * tc + gr * ts).astype(BF16)


def _fft3_kernel(x_ref, c_ref, s_ref, o_ref):
    c3 = c_ref[...]
    s3 = s_ref[...]
    for i in range(FFT3_ROWS):
        hr = x_ref[i, :, :FOURIER_WIDTH]
        hi = x_ref[i, :, FOURIER_WIDTH:]
        f = _dot(c3, hr) - _dot(s3, hi)
        o_ref[:, i * FOURIER_WIDTH:(i + 1) * FOURIER_WIDTH] = f.astype(BF16)


def _seq_dft(ab, w1, tc, ts, c3, s3, *, batch, seq):
    n2 = seq // FFT_N1
    wd = 2 * FOURIER_WIDTH
    x = ab.reshape(batch, FFT_N1, n2 * wd)
    blk = FFT1_COLS * wd
    h = pl.pallas_call(
        _fft1_kernel,
        grid=(batch, n2 // FFT1_COLS),
        in_specs=[pl.BlockSpec((None, FFT_N1, blk), lambda b, j: (b, 0, j)),
                  _const_spec(w1.shape),
                  pl.BlockSpec((FFT1_COLS, FFT_N1, 128), lambda b, j: (j, 0, 0)),
                  pl.BlockSpec((FFT1_COLS, FFT_N1, 128), lambda b, j: (j, 0, 0))],
        out_specs=pl.BlockSpec((None, FFT_N1, blk), lambda b, j: (b, 0, j)),
        out_shape=jax.ShapeDtypeStruct((batch, FFT_N1, n2 * wd), BF16),
        compiler_params=_params(2),
        name="fft1",
    )(x, w1, tc, ts)
    h = h.reshape(batch, FFT_N1, n2, wd)
    oblk = FFT3_ROWS * FOURIER_WIDTH
    f = pl.pallas_call(
        _fft3_kernel,
        grid=(batch, FFT_N1 // FFT3_ROWS),
        in_specs=[pl.BlockSpec((None, FFT3_ROWS, n2, wd), lambda b, j: (b, j, 0, 0)),
                  _const_spec(c3.shape), _const_spec(s3.shape)],
        out_specs=pl.BlockSpec((None, n2, oblk), lambda b, j: (b, 0, j)),
        out_shape=jax.ShapeDtypeStruct((batch, n2, FFT_N1 * FOURIER_WIDTH), BF16),
        compiler_params=_params(2),
        name="fft3",
    )(h, c3, s3)
    return f.reshape(batch * seq, FOURIER_WIDTH)


def _dft_constants(seq):
    n1, n2 = FFT_N1, seq // FFT_N1
    gd = FOURIER_GROUP_DIM
    i = np.arange(gd)
    ang_c = 2.0 * np.pi * np.outer(i, i) / gd
    cs = np.concatenate([np.cos(ang_c), np.sin(ang_c)], axis=1)
    k1 = np.arange(n1)
    ang1 = 2.0 * np.pi * np.outer(k1, k1) / n1
    c1, s1 = np.cos(ang1), np.sin(ang1)
    w1 = np.block([[c1, -s1], [s1, c1]])
    s2 = np.arange(n2)
    ang_t = 2.0 * np.pi * np.outer(s2, k1) / seq
    ang3 = 2.0 * np.pi * np.outer(s2, s2) / n2
    norm = 1.0 / np.sqrt(float(seq) * gd)
    f32 = lambda a: jnp.asarray(np.asarray(a, np.float32))
    tc = jnp.broadcast_to(f32(np.cos(ang_t))[:, :, None], (n2, n1, 128))
    ts = jnp.broadcast_to(f32(np.sin(ang_t))[:, :, None], (n2, n1, 128))
    return (f32(cs).astype(BF16), f32(w1).astype(BF16), tc, ts,
            f32(np.cos(ang3) * norm).astype(BF16), f32(np.sin(ang3) * norm).astype(BF16))


def _attn_kernel(qt_ref, k_ref, vt_ref, o_ref):
    qt = qt_ref[...]
    tq = qt.shape[1]
    n_chunks = vt_ref.shape[0]
    ones = jnp.ones((16, KV_TILE), BF16)

    def body(j, carry):
        m, acc = carry
        kc = k_ref[pl.ds(pl.multiple_of(j * KV_TILE, KV_TILE), KV_TILE), :]
        s = _dot(kc, qt)
        m_new = jnp.maximum(m, jnp.max(s, axis=0, keepdims=True))
        p = jnp.exp2(s - m_new).astype(BF16)
        alpha = jnp.exp2(m - m_new)
        vc = jnp.concatenate([vt_ref[j], ones], axis=0)
        return m_new, alpha * acc + _dot(vc, p)

    m0 = jnp.full((1, tq), -jnp.inf, F32)
    acc0 = jnp.zeros((V_HEAD_DIM + 16, tq), F32)
    _, acc = lax.fori_loop(0, n_chunks, body, (m0, acc0))
    o_ref[...] = (acc[:V_HEAD_DIM] / acc[V_HEAD_DIM:V_HEAD_DIM + 1]).astype(BF16)


def _attention(qt, k, vt, *, batch, seq):
    n_chunks = seq // KV_TILE
    return pl.pallas_call(
        _attn_kernel,
        grid=(batch, MLA_HEADS, seq // Q_TILE),
        in_specs=[pl.BlockSpec((None, HEAD_PAD, Q_TILE), lambda b, h, i: (b, h, i)),
                  pl.BlockSpec((None, seq, HEAD_PAD), lambda b, h, i: (b, 0, h)),
                  pl.BlockSpec((None, None, n_chunks, V_HEAD_DIM, KV_TILE),
                               lambda b, h, i: (b, h, 0, 0, 0))],
        out_specs=pl.BlockSpec((None, V_HEAD_DIM, Q_TILE), lambda b, h, i: (b, h, i)),
        out_shape=jax.ShapeDtypeStruct((batch, MLA_HEADS * V_HEAD_DIM, seq), BF16),
        compiler_params=_params(3),
        name="attn",
    )(qt, k.reshape(batch, seq, MLA_HEADS * HEAD_PAD), vt)


def _merge_kernel(x_ref, mod_ref, g_ref, f_ref, ot_ref, wg_ref, wfo_ref, wmo_ref, wout_ref, o_ref):
    x = x_ref[...]
    d = x.shape[1]
    h = _norm_mod(x, g_ref[...], mod_ref[3:4, :], mod_ref[4:5, :]).astype(BF16)
    gl = _dot(h, wg_ref[...])
    y_a = _dot(f_ref[...], wfo_ref[...])
    y_b = _dot_tn(ot_ref[...], wmo_ref[...])
    y = jax.nn.sigmoid(gl[:, :d]) * y_a + jax.nn.sigmoid(gl[:, d:]) * y_b
    o_ref[...] = x + mod_ref[5:6, :] * _dot(y.astype(BF16), wout_ref[...])


def _merge(x2d, mod, norm_g, f, ot, w_g, w_fo, w_mo, w_out, *, seq):
    n, d = x2d.shape
    per_batch = seq // ROW_TILE
    hv = MLA_HEADS * V_HEAD_DIM
    return pl.pallas_call(
        _merge_kernel,
        grid=(n // ROW_TILE,),
        in_specs=[pl.BlockSpec((ROW_TILE, d), lambda i: (i, 0)),
                  pl.BlockSpec((None, N_ADA, d), lambda i: (i // per_batch, 0, 0)),
                  _const_spec((1, d)),
                  pl.BlockSpec((ROW_TILE, FOURIER_WIDTH), lambda i: (i, 0)),
                  pl.BlockSpec((None, hv, ROW_TILE), lambda i: (i // per_batch, 0, i % per_batch)),
                  _const_spec(w_g.shape), _const_spec(w_fo.shape), _const_spec(w_mo.shape),
                  _const_spec(w_out.shape)],
        out_specs=pl.BlockSpec((ROW_TILE, d), lambda i: (i, 0)),
        out_shape=jax.ShapeDtypeStruct((n, d), F32),
        compiler_params=_params(1),
        name="merge",
    )(x2d, mod, norm_g.reshape(1, d), f, ot, w_g, w_fo, w_mo, w_out)


def kernel(x, c, positions, ada_w, ada_b, ffn1_norm, ffn1_w_gate, ffn1_w_up, ffn1_w_down, mix_norm,
           w_in, q_norm, w_q_up, kv_norm, w_kv_up, w_fourier_out, w_mla_out, w_out, ffn2_norm,
           ffn2_w_gate, ffn2_w_up, ffn2_w_down, final_norm):
    batch, seq, d = x.shape
    depth = ada_w.shape[0]
    x2d = x.reshape(batch * seq, d)
    pos_rows = positions.astype(F32).reshape(batch * seq // ROW_TILE, 1, ROW_TILE)
    inv_freq = (1.0 / (ROPE_THETA ** (jnp.arange(ROPE_HALF, dtype=F32) * 2.0 / QK_ROPE_DIM)))
    inv_freq = inv_freq.reshape(ROPE_HALF, 1)
    q_scale = float(QK_DIM ** -0.5 * np.log2(np.e))
    cs, w1, tc, ts, c3, s3 = _dft_constants(seq)
    s1 = FOURIER_WIDTH
    s2 = s1 + Q_LORA_RANK
    s3_ = s2 + KV_LORA_RANK
    s4 = s3_ + QK_ROPE_DIM

    for l in range(depth):
        mod = _ada(c, ada_w[l], ada_b[l])
        bf = lambda a: a.astype(BF16)
        x2d = _ffn(x2d, mod, ffn1_norm[l], bf(ffn1_w_gate[l]), bf(ffn1_w_up[l]), bf(ffn1_w_down[l]),
                   final_norm, sub=0, final=False, seq=seq)

        w_in_l = w_in[l]
        w_qt = w_q_up[l].T.reshape(MLA_HEADS, QK_DIM, Q_LORA_RANK).reshape(MLA_HEADS * QK_DIM, -1)
        w_kv = w_kv_up[l].reshape(KV_LORA_RANK, MLA_HEADS, QK_NOPE_DIM + V_HEAD_DIM)
        w_knt = w_kv[:, :, :QK_NOPE_DIM].reshape(KV_LORA_RANK, -1).T
        w_vt = w_kv[:, :, QK_NOPE_DIM:].reshape(KV_LORA_RANK, -1).T
        ab, qt, k, vt = _mix_in(
            x2d, mod, mix_norm[l], pos_rows, inv_freq, bf(w_in_l[:, :s3_]), bf(w_in_l[:, s3_:s4].T),
            q_norm[l], kv_norm[l], bf(w_qt), bf(w_knt), bf(w_vt), cs,
            batch=batch, seq=seq, q_scale=q_scale)
        f = _seq_dft(ab, w1, tc, ts, c3, s3, batch=batch, seq=seq)
        ot = _attention(qt, k, vt, batch=batch, seq=seq)
        x2d = _merge(x2d, mod, mix_norm[l], f, ot, bf(w_in_l[:, s4:]), bf(w_fourier_out[l]),
                     bf(w_mla_out[l]), bf(w_out[l]), seq=seq)

        x2d = _ffn(x2d, mod, ffn2_norm[l], bf(ffn2_w_gate[l]), bf(ffn2_w_up[l]), bf(ffn2_w_down[l]),
                   final_norm, sub=2, final=(l == depth - 1), seq=seq)
    return x2d.reshape(batch, seq, d)
```

```python
import functools

import numpy as np
import jax
import jax.numpy as jnp
from jax import lax
from jax.experimental import pallas as pl
from jax.experimental.pallas import tpu as pltpu

F32 = jnp.float32
BF16 = jnp.bfloat16

FOURIER_GROUPS = 4
FOURIER_GROUP_DIM = 128
FOURIER_WIDTH = FOURIER_GROUPS * FOURIER_GROUP_DIM
MLA_HEADS = 8
QK_NOPE_DIM = 64
QK_ROPE_DIM = 32
QK_DIM = QK_NOPE_DIM + QK_ROPE_DIM
V_HEAD_DIM = 64
Q_LORA_RANK = 384
KV_LORA_RANK = 256
ROPE_THETA = 10000.0
NORM_EPS = 1e-6
N_ADA = 9

HEAD_PAD = 128
ROPE_HALF = QK_ROPE_DIM // 2
FFT_N1 = 128
ROW_TILE = 512
KV_TILE = 512
Q_TILE = 512
FFT1_COLS = 8
FFT3_ROWS = 8
VMEM_LIMIT = 56 * 1024 * 1024


def _params(n_axes):
    return pltpu.CompilerParams(
        dimension_semantics=("parallel",) * n_axes, vmem_limit_bytes=VMEM_LIMIT)


def _const_spec(shape):
    nd = len(shape)
    return pl.BlockSpec(shape, lambda *_: (0,) * nd, pipeline_mode=pl.Buffered(1))


def _rms(x, g):
    ms = jnp.mean(x * x, axis=-1, keepdims=True)
    return x * lax.rsqrt(ms + NORM_EPS) * g


def _norm_mod(x, g, shift, scale):
    return _rms(x, g) * (1.0 + scale) + shift


def _dot(a, b):
    return jnp.dot(a, b, preferred_element_type=F32)


def _dot_nt(a, b):
    return lax.dot_general(a, b, (((1,), (1,)), ((), ())), preferred_element_type=F32)


def _dot_tn(a, b):
    return lax.dot_general(a, b, (((0,), (0,)), ((), ())), preferred_element_type=F32)


def _ada_kernel(c_ref, w_ref, b_ref, o_ref):
    c = c_ref[...]
    ca = c * jax.nn.sigmoid(c)
    o_ref[...] = jnp.dot(ca, w_ref[...], preferred_element_type=F32,
                         precision=lax.Precision.HIGHEST) + b_ref[...]


def _ada(c, ada_w, ada_b):
    b, d = c.shape
    rows = 8
    c_pad = jnp.zeros((rows, d), F32).at[:b].set(c)
    out = pl.pallas_call(
        _ada_kernel,
        grid=(N_ADA,),
        in_specs=[pl.BlockSpec((rows, d), lambda j: (0, 0)),
                  pl.BlockSpec((d, d), lambda j: (0, j)),
                  pl.BlockSpec((1, d), lambda j: (0, j))],
        out_specs=pl.BlockSpec((rows, d), lambda j: (0, j)),
        out_shape=jax.ShapeDtypeStruct((rows, N_ADA * d), F32),
        compiler_params=_params(1),
        name="ada",
    )(c_pad, ada_w, ada_b.reshape(1, -1))
    return out[:b].reshape(b, N_ADA, d)


def _ffn_kernel(x_ref, mod_ref, g_ref, wg_ref, wu_ref, wd_ref, fn_ref, o_ref, *, sub, final):
    x = x_ref[...]
    shift = mod_ref[3 * sub:3 * sub + 1, :]
    scale = mod_ref[3 * sub + 1:3 * sub + 2, :]
    gate = mod_ref[3 * sub + 2:3 * sub + 3, :]
    h = _norm_mod(x, g_ref[...], shift, scale).astype(BF16)
    gg = _dot(h, wg_ref[...])
    uu = _dot(h, wu_ref[...])
    a = (gg * jax.nn.sigmoid(gg) * uu).astype(BF16)
    y = x + (0.5 * gate) * _dot(a, wd_ref[...])
    if final:
        y = _rms(y, fn_ref[...])
    o_ref[...] = y


def _ffn(x2d, mod, norm_g, wg, wu, wd, final_g, *, sub, final, seq):
    n, d = x2d.shape
    dff = wg.shape[1]
    per_batch = seq // ROW_TILE
    return pl.pallas_call(
        functools.partial(_ffn_kernel, sub=sub, final=final),
        grid=(n // ROW_TILE,),
        in_specs=[pl.BlockSpec((ROW_TILE, d), lambda i: (i, 0)),
                  pl.BlockSpec((None, N_ADA, d), lambda i: (i // per_batch, 0, 0)),
                  _const_spec((1, d)),
                  _const_spec((d, dff)), _const_spec((d, dff)), _const_spec((dff, d)),
                  _const_spec((1, d))],
        out_specs=pl.BlockSpec((ROW_TILE, d), lambda i: (i, 0)),
        out_shape=jax.ShapeDtypeStruct((n, d), F32),
        compiler_params=_params(1),
        name="ffn%d" % sub,
    )(x2d, mod, norm_g.reshape(1, d), wg, wu, wd, final_g.reshape(1, d))


def _mix_in_kernel(x_ref, mod_ref, g_ref, pos_ref, invf_ref, w_in_ref, w_krt_ref, qn_ref, kvn_ref,
                   w_qt_ref, w_knt_ref, w_vt_ref, cs_ref,
                   ab_ref, qt_ref, k_ref, vt_ref, *, q_scale):
    x = x_ref[...]
    h = _norm_mod(x, g_ref[...], mod_ref[3:4, :], mod_ref[4:5, :]).astype(BF16)
    z = _dot(h, w_in_ref[...])
    tile = x.shape[0]

    cs = cs_ref[...]
    for g in range(FOURIER_GROUPS):
        lo = g * FOURIER_GROUP_DIM
        u = z[:, lo:lo + FOURIER_GROUP_DIM].astype(BF16)
        ab = _dot(u, cs)
        ab_ref[:, lo:lo + FOURIER_GROUP_DIM] = ab[:, :FOURIER_GROUP_DIM].astype(BF16)
        ab_ref[:, FOURIER_WIDTH + lo:FOURIER_WIDTH + lo + FOURIER_GROUP_DIM] = (
            ab[:, FOURIER_GROUP_DIM:].astype(BF16))

    ang = invf_ref[...] * pos_ref[...]
    cos_t = jnp.cos(ang)
    sin_t = jnp.sin(ang)

    def rope_t(v):
        v1, v2 = v[:ROPE_HALF], v[ROPE_HALF:]
        return v1 * cos_t - v2 * sin_t, v2 * cos_t + v1 * sin_t

    pad = jnp.zeros((HEAD_PAD - QK_DIM, tile), F32)

    q_lat = z[:, FOURIER_WIDTH:FOURIER_WIDTH + Q_LORA_RANK]
    qn = _rms(q_lat, qn_ref[...]).astype(BF16)
    qt = _dot_nt(w_qt_ref[...], qn)
    for hd in range(MLA_HEADS):
        lo = hd * QK_DIM
        r1, r2 = rope_t(qt[lo + QK_NOPE_DIM:lo + QK_DIM])
        qh = jnp.concatenate([qt[lo:lo + QK_NOPE_DIM], r1, r2], axis=0) * q_scale
        qt_ref[hd * HEAD_PAD:(hd + 1) * HEAD_PAD, :] = (
            jnp.concatenate([qh, pad], axis=0).astype(BF16))

    kv_lat = z[:, FOURIER_WIDTH + Q_LORA_RANK:]
    kvn = _rms(kv_lat, kvn_ref[...]).astype(BF16)
    knt = _dot_nt(w_knt_ref[...], kvn)
    vt = _dot_nt(w_vt_ref[...], kvn)
    kr1, kr2 = rope_t(_dot_nt(w_krt_ref[...], h))
    for hd in range(MLA_HEADS):
        lo = hd * QK_NOPE_DIM
        kt = jnp.concatenate([knt[lo:lo + QK_NOPE_DIM], kr1, kr2, pad], axis=0)
        k_ref[:, hd * HEAD_PAD:(hd + 1) * HEAD_PAD] = kt.T.astype(BF16)
        vt_ref[hd, 0] = vt[hd * V_HEAD_DIM:(hd + 1) * V_HEAD_DIM].astype(BF16)


def _mix_in(x2d, mod, norm_g, pos_rows, inv_freq, w_in_a, w_krt, q_norm, kv_norm, w_qt, w_knt, w_vt,
            cs, *, batch, seq, q_scale):
    n, d = x2d.shape
    per_batch = seq // ROW_TILE
    assert ROW_TILE == KV_TILE
    ispec = [pl.BlockSpec((ROW_TILE, d), lambda i: (i, 0)),
             pl.BlockSpec((None, N_ADA, d), lambda i: (i // per_batch, 0, 0)),
             _const_spec((1, d)),
             pl.BlockSpec((None, 1, ROW_TILE), lambda i: (i, 0, 0)),
             _const_spec(inv_freq.shape),
             _const_spec(w_in_a.shape), _const_spec(w_krt.shape),
             _const_spec((1, Q_LORA_RANK)), _const_spec((1, KV_LORA_RANK)),
             _const_spec(w_qt.shape), _const_spec(w_knt.shape), _const_spec(w_vt.shape),
             _const_spec(cs.shape)]
    hp = MLA_HEADS * HEAD_PAD
    ospec = [pl.BlockSpec((ROW_TILE, 2 * FOURIER_WIDTH), lambda i: (i, 0)),
             pl.BlockSpec((None, hp, ROW_TILE), lambda i: (i // per_batch, 0, i % per_batch)),
             pl.BlockSpec((ROW_TILE, hp), lambda i: (i, 0)),
             pl.BlockSpec((None, MLA_HEADS, 1, V_HEAD_DIM, KV_TILE),
                          lambda i: (i // per_batch, 0, i % per_batch, 0, 0))]
    oshape = [jax.ShapeDtypeStruct((n, 2 * FOURIER_WIDTH), BF16),
              jax.ShapeDtypeStruct((batch, hp, seq), BF16),
              jax.ShapeDtypeStruct((n, hp), BF16),
              jax.ShapeDtypeStruct((batch, MLA_HEADS, seq // KV_TILE, V_HEAD_DIM, KV_TILE), BF16)]
    return pl.pallas_call(
        functools.partial(_mix_in_kernel, q_scale=q_scale),
        grid=(n // ROW_TILE,),
        in_specs=ispec, out_specs=ospec, out_shape=oshape,
        compiler_params=_params(1),
        name="mix_in",
    )(x2d, mod, norm_g.reshape(1, d), pos_rows, inv_freq, w_in_a, w_krt,
      q_norm.reshape(1, -1), kv_norm.reshape(1, -1), w_qt, w_knt, w_vt, cs)


def _fft1_kernel(x_ref, w_ref, tc_ref, ts_ref, o_ref):
    w = w_ref[...]
    wd = 2 * FOURIER_WIDTH
    for j in range(FFT1_COLS):
        a = x_ref[:, j * wd:j * wd + FOURIER_WIDTH]
        b = x_ref[:, j * wd + FOURIER_WIDTH:(j + 1) * wd]
        g = _dot(w, jnp.concatenate([a, b], axis=0))
        gr, gi = g[:FFT_N1], g[FFT_N1:]
        tc = jnp.concatenate([tc_ref[j]] * FOURIER_GROUPS, axis=1)
        ts = jnp.concatenate([ts_ref[j]] * FOURIER_GROUPS, axis=1)
        o_ref[:, j * wd:j * wd + FOURIER_WIDTH] = (gr * tc - gi * ts).astype(BF16)
        o_ref[:, j * wd + FOURIER_WIDTH:(j + 1) * wd] = (gi * tc + gr * ts).astype(BF16)


def _fft3_kernel(x_ref, c_ref, s_ref, o_ref):
    c3 = c_ref[...]
    s3 = s_ref[...]
    for i in range(FFT3_ROWS):
        hr = x_ref[i, :, :FOURIER_WIDTH]
        hi = x_ref[i, :, FOURIER_WIDTH:]
        f = _dot(c3, hr) - _dot(s3, hi)
        o_ref[:, i * FOURIER_WIDTH:(i + 1) * FOURIER_WIDTH] = f.astype(BF16)


def _seq_dft(ab, w1, tc, ts, c3, s3, *, batch, seq):
    n2 = seq // FFT_N1
    wd = 2 * FOURIER_WIDTH
    x = ab.reshape(batch, FFT_N1, n2 * wd)
    blk = FFT1_COLS * wd
    h = pl.pallas_call(
        _fft1_kernel,
        grid=(batch, n2 // FFT1_COLS),
        in_specs=[pl.BlockSpec((None, FFT_N1, blk), lambda b, j: (b, 0, j)),
                  _const_spec(w1.shape),
                  pl.BlockSpec((FFT1_COLS, FFT_N1, 128), lambda b, j: (j, 0, 0)),
                  pl.BlockSpec((FFT1_COLS, FFT_N1, 128), lambda b, j: (j, 0, 0))],
        out_specs=pl.BlockSpec((None, FFT_N1, blk), lambda b, j: (b, 0, j)),
        out_shape=jax.ShapeDtypeStruct((batch, FFT_N1, n2 * wd), BF16),
        compiler_params=_params(2),
        name="fft1",
    )(x, w1, tc, ts)
    h = h.reshape(batch, FFT_N1, n2, wd)
    oblk = FFT3_ROWS * FOURIER_WIDTH
    f = pl.pallas_call(
        _fft3_kernel,
        grid=(batch, FFT_N1 // FFT3_ROWS),
        in_specs=[pl.BlockSpec((None, FFT3_ROWS, n2, wd), lambda b, j: (b, j, 0, 0)),
                  _const_spec(c3.shape), _const_spec(s3.shape)],
        out_specs=pl.BlockSpec((None, n2, oblk), lambda b, j: (b, 0, j)),
        out_shape=jax.ShapeDtypeStruct((batch, n2, FFT_N1 * FOURIER_WIDTH), BF16),
        compiler_params=_params(2),
        name="fft3",
    )(h, c3, s3)
    return f.reshape(batch * seq, FOURIER_WIDTH)


def _dft_constants(seq):
    n1, n2 = FFT_N1, seq // FFT_N1
    gd = FOURIER_GROUP_DIM
    i = np.arange(gd)
    ang_c = 2.0 * np.pi * np.outer(i, i) / gd
    cs = np.concatenate([np.cos(ang_c), np.sin(ang_c)], axis=1)
    k1 = np.arange(n1)
    ang1 = 2.0 * np.pi * np.outer(k1, k1) / n1
    c1, s1 = np.cos(ang1), np.sin(ang1)
    w1 = np.block([[c1, -s1], [s1, c1]])
    s2 = np.arange(n2)
    ang_t = 2.0 * np.pi * np.outer(s2, k1) / seq
    ang3 = 2.0 * np.pi * np.outer(s2, s2) / n2
    norm = 1.0 / np.sqrt(float(seq) * gd)
    f32 = lambda a: jnp.asarray(np.asarray(a, np.float32))
    tc = jnp.broadcast_to(f32(np.cos(ang_t))[:, :, None], (n2, n1, 128))
    ts = jnp.broadcast_to(f32(np.sin(ang_t))[:, :, None], (n2, n1, 128))
    return (f32(cs).astype(BF16), f32(w1).astype(BF16), tc, ts,
            f32(np.cos(ang3) * norm).astype(BF16), f32(np.sin(ang3) * norm).astype(BF16))


def _attn_kernel(qt_ref, k_ref, vt_ref, o_ref, s_a, s_b):
    qt = qt_ref[...]
    tq = qt.shape[1]
    n_chunks = vt_ref.shape[0]
    ones = jnp.ones((16, KV_TILE), BF16)

    def scores(j, s_ref):
        kc = k_ref[pl.ds(pl.multiple_of(j * KV_TILE, KV_TILE), KV_TILE), :]
        s = _dot(kc, qt)
        s_ref[...] = s
        return jnp.max(s, axis=0, keepdims=True)

    def consume(j, s_ref, m, acc, cmax):
        m_new = jnp.maximum(m, cmax)
        p = jnp.exp2(s_ref[...] - m_new).astype(BF16)
        alpha = jnp.exp2(m - m_new)
        vc = jnp.concatenate([vt_ref[j], ones], axis=0)
        return m_new, alpha * acc + _dot(vc, p)

    m = jnp.full((1, tq), -jnp.inf, F32)
    acc = jnp.zeros((V_HEAD_DIM + 16, tq), F32)
    bufs = (s_a, s_b)
    cmax = scores(0, s_a)
    for j in range(n_chunks):
        cmax_next = scores(j + 1, bufs[(j + 1) % 2]) if j + 1 < n_chunks else None
        m, acc = consume(j, bufs[j % 2], m, acc, cmax)
        cmax = cmax_next
    o_ref[...] = (acc[:V_HEAD_DIM] / acc[V_HEAD_DIM:V_HEAD_DIM + 1]).astype(BF16)


def _attention(qt, k, vt, *, batch, seq):
    n_chunks = seq // KV_TILE
    return pl.pallas_call(
        _attn_kernel,
        grid=(batch, MLA_HEADS, seq // Q_TILE),
        in_specs=[pl.BlockSpec((None, HEAD_PAD, Q_TILE), lambda b, h, i: (b, h, i)),
                  pl.BlockSpec((None, seq, HEAD_PAD), lambda b, h, i: (b, 0, h)),
                  pl.BlockSpec((None, None, n_chunks, V_HEAD_DIM, KV_TILE),
                               lambda b, h, i: (b, h, 0, 0, 0))],
        out_specs=pl.BlockSpec((None, V_HEAD_DIM, Q_TILE), lambda b, h, i: (b, h, i)),
        out_shape=jax.ShapeDtypeStruct((batch, MLA_HEADS * V_HEAD_DIM, seq), BF16),
        scratch_shapes=[pltpu.VMEM((KV_TILE, Q_TILE), F32), pltpu.VMEM((KV_TILE, Q_TILE), F32)],
        compiler_params=_params(3),
        name="attn",
    )(qt, k.reshape(batch, seq, MLA_HEADS * HEAD_PAD), vt)


def _merge_kernel(x_ref, mod_ref, g_ref, f_ref, ot_ref, wg_ref, wfo_ref, wmo_ref, wout_ref, o_ref):
    x = x_ref[...]
    d = x.shape[1]
    h = _norm_mod(x, g_ref[...], mod_ref[3:4, :], mod_ref[4:5, :]).astype(BF16)
    gl = _dot(h, wg_ref[...])
    y_a = _dot(f_ref[...], wfo_ref[...])
    y_b = _dot_tn(ot_ref[...], wmo_ref[...])
    y = jax.nn.sigmoid(gl[:, :d]) * y_a + jax.nn.sigmoid(gl[:, d:]) * y_b
    o_ref[...] = x + mod_ref[5:6, :] * _dot(y.astype(BF16), wout_ref[...])


def _merge(x2d, mod, norm_g, f, ot, w_g, w_fo, w_mo, w_out, *, seq):
    n, d = x2d.shape
    per_batch = seq // ROW_TILE
    hv = MLA_HEADS * V_HEAD_DIM
    return pl.pallas_call(
        _merge_kernel,
        grid=(n // ROW_TILE,),
        in_specs=[pl.BlockSpec((ROW_TILE, d), lambda i: (i, 0)),
                  pl.BlockSpec((None, N_ADA, d), lambda i: (i // per_batch, 0, 0)),
                  _const_spec((1, d)),
                  pl.BlockSpec((ROW_TILE, FOURIER_WIDTH), lambda i: (i, 0)),
                  pl.BlockSpec((None, hv, ROW_TILE), lambda i: (i // per_batch, 0, i % per_batch)),
                  _const_spec(w_g.shape), _const_spec(w_fo.shape), _const_spec(w_mo.shape),
                  _const_spec(w_out.shape)],
        out_specs=pl.BlockSpec((ROW_TILE, d), lambda i: (i, 0)),
        out_shape=jax.ShapeDtypeStruct((n, d), F32),
        compiler_params=_params(1),
        name="merge",
    )(x2d, mod, norm_g.reshape(1, d), f, ot, w_g, w_fo, w_mo, w_out)


def kernel(x, c, positions, ada_w, ada_b, ffn1_norm, ffn1_w_gate, ffn1_w_up, ffn1_w_down, mix_norm,
           w_in, q_norm, w_q_up, kv_norm, w_kv_up, w_fourier_out, w_mla_out, w_out, ffn2_norm,
           ffn2_w_gate, ffn2_w_up, ffn2_w_down, final_norm):
    batch, seq, d = x.shape
    depth = ada_w.shape[0]
    x2d = x.reshape(batch * seq, d)
    pos_rows = positions.astype(F32).reshape(batch * seq // ROW_TILE, 1, ROW_TILE)
    inv_freq = (1.0 / (ROPE_THETA ** (jnp.arange(ROPE_HALF, dtype=F32) * 2.0 / QK_ROPE_DIM)))
    inv_freq = inv_freq.reshape(ROPE_HALF, 1)
    q_scale = float(QK_DIM ** -0.5 * np.log2(np.e))
    cs, w1, tc, ts, c3, s3 = _dft_constants(seq)
    s1 = FOURIER_WIDTH
    s2 = s1 + Q_LORA_RANK
    s3_ = s2 + KV_LORA_RANK
    s4 = s3_ + QK_ROPE_DIM

    for l in range(depth):
        mod = _ada(c, ada_w[l], ada_b[l])
        bf = lambda a: a.astype(BF16)
        x2d = _ffn(x2d, mod, ffn1_norm[l], bf(ffn1_w_gate[l]), bf(ffn1_w_up[l]), bf(ffn1_w_down[l]),
                   final_norm, sub=0, final=False, seq=seq)

        w_in_l = w_in[l]
        w_qt = w_q_up[l].T.reshape(MLA_HEADS, QK_DIM, Q_LORA_RANK).reshape(MLA_HEADS * QK_DIM, -1)
        w_kv = w_kv_up[l].reshape(KV_LORA_RANK, MLA_HEADS, QK_NOPE_DIM + V_HEAD_DIM)
        w_knt = w_kv[:, :, :QK_NOPE_DIM].reshape(KV_LORA_RANK, -1).T
        w_vt = w_kv[:, :, QK_NOPE_DIM:].reshape(KV_LORA_RANK, -1).T
        ab, qt, k, vt = _mix_in(
            x2d, mod, mix_norm[l], pos_rows, inv_freq, bf(w_in_l[:, :s3_]), bf(w_in_l[:, s3_:s4].T),
            q_norm[l], kv_norm[l], bf(w_qt), bf(w_knt), bf(w_vt), cs,
            batch=batch, seq=seq, q_scale=q_scale)
        f = _seq_dft(ab, w1, tc, ts, c3, s3, batch=batch, seq=seq)
        ot = _attention(qt, k, vt, batch=batch, seq=seq)
        x2d = _merge(x2d, mod, mix_norm[l], f, ot, bf(w_in_l[:, s4:]), bf(w_fourier_out[l]),
                     bf(w_mla_out[l]), bf(w_out[l]), seq=seq)

        x2d = _ffn(x2d, mod, ffn2_norm[l], bf(ffn2_w_gate[l]), bf(ffn2_w_up[l]), bf(ffn2_w_down[l]),
                   final_norm, sub=2, final=(l == depth - 1), seq=seq)
    return x2d.reshape(batch, seq, d)
```

```python
import functools

import numpy as np
import jax
import jax.numpy as jnp
from jax import lax
from jax.experimental import pallas as pl
from jax.experimental.pallas import tpu as pltpu

F32 = jnp.float32
BF16 = jnp.bfloat16

FOURIER_GROUPS = 4
FOURIER_GROUP_DIM = 128
FOURIER_WIDTH = FOURIER_GROUPS * FOURIER_GROUP_DIM
MLA_HEADS = 8
QK_NOPE_DIM = 64
QK_ROPE_DIM = 32
QK_DIM = QK_NOPE_DIM + QK_ROPE_DIM
V_HEAD_DIM = 64
Q_LORA_RANK = 384
KV_LORA_RANK = 256
ROPE_THETA = 10000.0
NORM_EPS = 1e-6
N_ADA = 9

HEAD_PAD = 128
ROPE_HALF = QK_ROPE_DIM // 2
FFT_N1 = 128
ROW_TILE = 512
KV_TILE = 256
Q_TILE = 512
ATTN_AHEAD = 2
ATTN_LAG = 2
ATTN_MAX_EXCESS = 60.0
FFT1_COLS = 8
FFT3_ROWS = 8
VMEM_LIMIT = 56 * 1024 * 1024


def _params(n_axes, flags=None):
    return pltpu.CompilerParams(
        dimension_semantics=("parallel",) * n_axes, vmem_limit_bytes=VMEM_LIMIT, flags=flags)


def _const_spec(shape):
    nd = len(shape)
    return pl.BlockSpec(shape, lambda *_: (0,) * nd, pipeline_mode=pl.Buffered(1))


def _rms(x, g):
    ms = jnp.mean(x * x, axis=-1, keepdims=True)
    return x * lax.rsqrt(ms + NORM_EPS) * g


def _norm_mod(x, g, shift, scale):
    return _rms(x, g) * (1.0 + scale) + shift


def _dot(a, b):
    return jnp.dot(a, b, preferred_element_type=F32)


def _dot_nt(a, b):
    return lax.dot_general(a, b, (((1,), (1,)), ((), ())), preferred_element_type=F32)


def _dot_tn(a, b):
    return lax.dot_general(a, b, (((0,), (0,)), ((), ())), preferred_element_type=F32)


def _ada_kernel(c_ref, w_ref, b_ref, o_ref):
    c = c_ref[...]
    ca = c * jax.nn.sigmoid(c)
    o_ref[...] = jnp.dot(ca, w_ref[...], preferred_element_type=F32,
                         precision=lax.Precision.HIGHEST) + b_ref[...]


def _ada(c, ada_w, ada_b):
    b, d = c.shape
    rows = 8
    c_pad = jnp.zeros((rows, d), F32).at[:b].set(c)
    out = pl.pallas_call(
        _ada_kernel,
        grid=(N_ADA,),
        in_specs=[pl.BlockSpec((rows, d), lambda j: (0, 0)),
                  pl.BlockSpec((d, d), lambda j: (0, j)),
                  pl.BlockSpec((1, d), lambda j: (0, j))],
        out_specs=pl.BlockSpec((rows, d), lambda j: (0, j)),
        out_shape=jax.ShapeDtypeStruct((rows, N_ADA * d), F32),
        compiler_params=_params(1),
        name="ada",
    )(c_pad, ada_w, ada_b.reshape(1, -1))
    return out[:b].reshape(b, N_ADA, d)


def _ffn_kernel(x_ref, mod_ref, g_ref, wg_ref, wu_ref, wd_ref, fn_ref, o_ref, *, sub, final):
    x = x_ref[...]
    shift = mod_ref[3 * sub:3 * sub + 1, :]
    scale = mod_ref[3 * sub + 1:3 * sub + 2, :]
    gate = mod_ref[3 * sub + 2:3 * sub + 3, :]
    h = _norm_mod(x, g_ref[...], shift, scale).astype(BF16)
    gg = _dot(h, wg_ref[...])
    uu = _dot(h, wu_ref[...])
    a = (gg * jax.nn.sigmoid(gg) * uu).astype(BF16)
    y = x + (0.5 * gate) * _dot(a, wd_ref[...])
    if final:
        y = _rms(y, fn_ref[...])
    o_ref[...] = y


def _ffn(x2d, mod, norm_g, wg, wu, wd, final_g, *, sub, final, seq):
    n, d = x2d.shape
    dff = wg.shape[1]
    per_batch = seq // ROW_TILE
    return pl.pallas_call(
        functools.partial(_ffn_kernel, sub=sub, final=final),
        grid=(n // ROW_TILE,),
        in_specs=[pl.BlockSpec((ROW_TILE, d), lambda i: (i, 0)),
                  pl.BlockSpec((None, N_ADA, d), lambda i: (i // per_batch, 0, 0)),
                  _const_spec((1, d)),
                  _const_spec((d, dff)), _const_spec((d, dff)), _const_spec((dff, d)),
                  _const_spec((1, d))],
        out_specs=pl.BlockSpec((ROW_TILE, d), lambda i: (i, 0)),
        out_shape=jax.ShapeDtypeStruct((n, d), F32),
        compiler_params=_params(1),
        name="ffn%d" % sub,
    )(x2d, mod, norm_g.reshape(1, d), wg, wu, wd, final_g.reshape(1, d))


def _mix_in_kernel(x_ref, mod_ref, g_ref, pos_ref, invf_ref, w_in_ref, w_krt_ref, qn_ref, kvn_ref,
                   w_qt_ref, w_knt_ref, w_vt_ref, cs_ref,
                   ab_ref, qt_ref, k_ref, vt_ref, *, q_scale):
    x = x_ref[...]
    h = _norm_mod(x, g_ref[...], mod_ref[3:4, :], mod_ref[4:5, :]).astype(BF16)
    z = _dot(h, w_in_ref[...])
    tile = x.shape[0]

    cs = cs_ref[...]
    for g in range(FOURIER_GROUPS):
        lo = g * FOURIER_GROUP_DIM
        u = z[:, lo:lo + FOURIER_GROUP_DIM].astype(BF16)
        ab = _dot(u, cs)
        ab_ref[:, lo:lo + FOURIER_GROUP_DIM] = ab[:, :FOURIER_GROUP_DIM].astype(BF16)
        ab_ref[:, FOURIER_WIDTH + lo:FOURIER_WIDTH + lo + FOURIER_GROUP_DIM] = (
            ab[:, FOURIER_GROUP_DIM:].astype(BF16))

    ang = invf_ref[...] * pos_ref[...]
    cos_t = jnp.cos(ang)
    sin_t = jnp.sin(ang)

    def rope_t(v):
        v1, v2 = v[:ROPE_HALF], v[ROPE_HALF:]
        return v1 * cos_t - v2 * sin_t, v2 * cos_t + v1 * sin_t

    pad = jnp.zeros((HEAD_PAD - QK_DIM, tile), F32)

    q_lat = z[:, FOURIER_WIDTH:FOURIER_WIDTH + Q_LORA_RANK]
    qn = _rms(q_lat, qn_ref[...]).astype(BF16)
    qt = _dot_nt(w_qt_ref[...], qn)
    for hd in range(MLA_HEADS):
        lo = hd * QK_DIM
        r1, r2 = rope_t(qt[lo + QK_NOPE_DIM:lo + QK_DIM])
        qh = jnp.concatenate([qt[lo:lo + QK_NOPE_DIM], r1, r2], axis=0) * q_scale
        qt_ref[hd * HEAD_PAD:(hd + 1) * HEAD_PAD, :] = (
            jnp.concatenate([qh, pad], axis=0).astype(BF16))

    kv_lat = z[:, FOURIER_WIDTH + Q_LORA_RANK:]
    kvn = _rms(kv_lat, kvn_ref[...]).astype(BF16)
    knt = _dot_nt(w_knt_ref[...], kvn)
    vt = _dot_nt(w_vt_ref[...], kvn)
    kr1, kr2 = rope_t(_dot_nt(w_krt_ref[...], h))
    for hd in range(MLA_HEADS):
        lo = hd * QK_NOPE_DIM
        kt = jnp.concatenate([knt[lo:lo + QK_NOPE_DIM], kr1, kr2, pad], axis=0)
        k_ref[:, hd * HEAD_PAD:(hd + 1) * HEAD_PAD] = kt.T.astype(BF16)
        for cch in range(tile // KV_TILE):
            vt_ref[hd, cch] = vt[hd * V_HEAD_DIM:(hd + 1) * V_HEAD_DIM,
                                 cch * KV_TILE:(cch + 1) * KV_TILE].astype(BF16)


def _mix_in(x2d, mod, norm_g, pos_rows, inv_freq, w_in_a, w_krt, q_norm, kv_norm, w_qt, w_knt, w_vt,
            cs, *, batch, seq, q_scale):
    n, d = x2d.shape
    per_batch = seq // ROW_TILE
    assert ROW_TILE % KV_TILE == 0
    chunks_per_tile = ROW_TILE // KV_TILE
    ispec = [pl.BlockSpec((ROW_TILE, d), lambda i: (i, 0)),
             pl.BlockSpec((None, N_ADA, d), lambda i: (i // per_batch, 0, 0)),
             _const_spec((1, d)),
             pl.BlockSpec((None, 1, ROW_TILE), lambda i: (i, 0, 0)),
             _const_spec(inv_freq.shape),
             _const_spec(w_in_a.shape), _const_spec(w_krt.shape),
             _const_spec((1, Q_LORA_RANK)), _const_spec((1, KV_LORA_RANK)),
             _const_spec(w_qt.shape), _const_spec(w_knt.shape), _const_spec(w_vt.shape),
             _const_spec(cs.shape)]
    hp = MLA_HEADS * HEAD_PAD
    ospec = [pl.BlockSpec((ROW_TILE, 2 * FOURIER_WIDTH), lambda i: (i, 0)),
             pl.BlockSpec((None, hp, ROW_TILE), lambda i: (i // per_batch, 0, i % per_batch)),
             pl.BlockSpec((ROW_TILE, hp), lambda i: (i, 0)),
             pl.BlockSpec((None, MLA_HEADS, chunks_per_tile, V_HEAD_DIM, KV_TILE),
                          lambda i: (i // per_batch, 0, i % per_batch, 0, 0))]
    oshape = [jax.ShapeDtypeStruct((n, 2 * FOURIER_WIDTH), BF16),
              jax.ShapeDtypeStruct((batch, hp, seq), BF16),
              jax.ShapeDtypeStruct((n, hp), BF16),
              jax.ShapeDtypeStruct((batch, MLA_HEADS, seq // KV_TILE, V_HEAD_DIM, KV_TILE), BF16)]
    return pl.pallas_call(
        functools.partial(_mix_in_kernel, q_scale=q_scale),
        grid=(n // ROW_TILE,),
        in_specs=ispec, out_specs=ospec, out_shape=oshape,
        compiler_params=_params(1),
        name="mix_in",
    )(x2d, mod, norm_g.reshape(1, d), pos_rows, inv_freq, w_in_a, w_krt,
      q_norm.reshape(1, -1), kv_norm.reshape(1, -1), w_qt, w_knt, w_vt, cs)


def _fft1_kernel(x_ref, w_ref, tc_ref, ts_ref, o_ref):
    w = w_ref[...]
    wd = 2 * FOURIER_WIDTH
    for j in range(FFT1_COLS):
        a = x_ref[:, j * wd:j * wd + FOURIER_WIDTH]
        b = x_ref[:, j * wd + FOURIER_WIDTH:(j + 1) * wd]
        g = _dot(w, jnp.concatenate([a, b], axis=0))
        gr, gi = g[:FFT_N1], g[FFT_N1:]
        tc = jnp.concatenate([tc_ref[j]] * FOURIER_GROUPS, axis=1)
        ts = jnp.concatenate([ts_ref[j]] * FOURIER_GROUPS, axis=1)
        o_ref[:, j * wd:j * wd + FOURIER_WIDTH] = (gr * tc - gi * ts).astype(BF16)
        o_ref[:, j * wd + FOURIER_WIDTH:(j + 1) * wd] = (gi * tc + gr * ts).astype(BF16)


def _fft3_kernel(x_ref, c_ref, s_ref, o_ref):
    c3 = c_ref[...]
    s3 = s_ref[...]
    for i in range(FFT3_ROWS):
        hr = x_ref[i, :, :FOURIER_WIDTH]
        hi = x_ref[i, :, FOURIER_WIDTH:]
        f = _dot(c3, hr) - _dot(s3, hi)
        o_ref[:, i * FOURIER_WIDTH:(i + 1) * FOURIER_WIDTH] = f.astype(BF16)


def _seq_dft(ab, w1, tc, ts, c3, s3, *, batch, seq):
    n2 = seq // FFT_N1
    wd = 2 * FOURIER_WIDTH
    x = ab.reshape(batch, FFT_N1, n2 * wd)
    blk = FFT1_COLS * wd
    h = pl.pallas_call(
        _fft1_kernel,
        grid=(batch, n2 // FFT1_COLS),
        in_specs=[pl.BlockSpec((None, FFT_N1, blk), lambda b, j: (b, 0, j)),
                  _const_spec(w1.shape),
                  pl.BlockSpec((FFT1_COLS, FFT_N1, 128), lambda b, j: (j, 0, 0)),
                  pl.BlockSpec((FFT1_COLS, FFT_N1, 128), lambda b, j: (j, 0, 0))],
        out_specs=pl.BlockSpec((None, FFT_N1, blk), lambda b, j: (b, 0, j)),
        out_shape=jax.ShapeDtypeStruct((batch, FFT_N1, n2 * wd), BF16),
        compiler_params=_params(2),
        name="fft1",
    )(x, w1, tc, ts)
    h = h.reshape(batch, FFT_N1, n2, wd)
    oblk = FFT3_ROWS * FOURIER_WIDTH
    f = pl.pallas_call(
        _fft3_kernel,
        grid=(batch, FFT_N1 // FFT3_ROWS),
        in_specs=[pl.BlockSpec((None, FFT3_ROWS, n2, wd), lambda b, j: (b, j, 0, 0)),
                  _const_spec(c3.shape), _const_spec(s3.shape)],
        out_specs=pl.BlockSpec((None, n2, oblk), lambda b, j: (b, 0, j)),
        out_shape=jax.ShapeDtypeStruct((batch, n2, FFT_N1 * FOURIER_WIDTH), BF16),
        compiler_params=_params(2),
        name="fft3",
    )(h, c3, s3)
    return f.reshape(batch * seq, FOURIER_WIDTH)


def _dft_constants(seq):
    n1, n2 = FFT_N1, seq // FFT_N1
    gd = FOURIER_GROUP_DIM
    i = np.arange(gd)
    ang_c = 2.0 * np.pi * np.outer(i, i) / gd
    cs = np.concatenate([np.cos(ang_c), np.sin(ang_c)], axis=1)
    k1 = np.arange(n1)
    ang1 = 2.0 * np.pi * np.outer(k1, k1) / n1
    c1, s1 = np.cos(ang1), np.sin(ang1)
    w1 = np.block([[c1, -s1], [s1, c1]])
    s2 = np.arange(n2)
    ang_t = 2.0 * np.pi * np.outer(s2, k1) / seq
    ang3 = 2.0 * np.pi * np.outer(s2, s2) / n2
    norm = 1.0 / np.sqrt(float(seq) * gd)
    f32 = lambda a: jnp.asarray(np.asarray(a, np.float32))
    tc = jnp.broadcast_to(f32(np.cos(ang_t))[:, :, None], (n2, n1, 128))
    ts = jnp.broadcast_to(f32(np.sin(ang_t))[:, :, None], (n2, n1, 128))
    return (f32(cs).astype(BF16), f32(w1).astype(BF16), tc, ts,
            f32(np.cos(ang3) * norm).astype(BF16), f32(np.sin(ang3) * norm).astype(BF16))


def _attn_kernel(qt_ref, k_ref, vt_ref, o_ref):
    qt = qt_ref[...]
    tq = qt.shape[1]
    n_chunks = vt_ref.shape[0]
    ones = jnp.ones((16, KV_TILE), BF16)

    def qk(j):
        return _dot(k_ref[j * KV_TILE:(j + 1) * KV_TILE, :], qt)

    def pv(j, p):
        return _dot(jnp.concatenate([vt_ref[j], ones], axis=0), p)

    s0 = qk(0)
    m = jnp.max(s0, axis=0, keepdims=True)
    inflight = [qk(j) for j in range(1, 1 + ATTN_AHEAD)]
    acc = pv(0, jnp.exp2(s0 - m).astype(BF16))
    excess = jnp.zeros((1, tq), F32)
    pending = []
    for j in range(1, n_chunks):
        s = inflight.pop(0)
        if j + ATTN_AHEAD < n_chunks:
            inflight.append(qk(j + ATTN_AHEAD))
        cm = jnp.max(s, axis=0, keepdims=True)
        acc = acc + pv(j, jnp.exp2(s - m).astype(BF16))
        excess = jnp.maximum(excess, cm - m)
        pending.append(cm)
        if len(pending) >= ATTN_LAG:
            m_new = jnp.maximum(m, pending.pop(0))
            acc = acc * jnp.exp2(m - m_new)
            m = m_new
    o_ref[...] = (acc[:V_HEAD_DIM] / acc[V_HEAD_DIM:V_HEAD_DIM + 1]).astype(BF16)

    @pl.when(jnp.max(excess) > ATTN_MAX_EXCESS)
    def _():
        def body(j, carry):
            m_c, acc_c = carry
            kc = k_ref[pl.ds(pl.multiple_of(j * KV_TILE, KV_TILE), KV_TILE), :]
            s_c = _dot(kc, qt)
            m_n = jnp.maximum(m_c, jnp.max(s_c, axis=0, keepdims=True))
            p_c = jnp.exp2(s_c - m_n).astype(BF16)
            return m_n, jnp.exp2(m_c - m_n) * acc_c + pv(j, p_c)

        m_0 = jnp.full((1, tq), -jnp.inf, F32)
        acc_0 = jnp.zeros((V_HEAD_DIM + 16, tq), F32)
        _, acc_s = lax.fori_loop(0, n_chunks, body, (m_0, acc_0))
        o_ref[...] = (acc_s[:V_HEAD_DIM] / acc_s[V_HEAD_DIM:V_HEAD_DIM + 1]).astype(BF16)


def _attention(qt, k, vt, *, batch, seq):
    n_chunks = seq // KV_TILE
    return pl.pallas_call(
        _attn_kernel,
        grid=(batch, MLA_HEADS, seq // Q_TILE),
        in_specs=[pl.BlockSpec((None, HEAD_PAD, Q_TILE), lambda b, h, i: (b, h, i)),
                  pl.BlockSpec((None, seq, HEAD_PAD), lambda b, h, i: (b, 0, h)),
                  pl.BlockSpec((None, None, n_chunks, V_HEAD_DIM, KV_TILE),
                               lambda b, h, i: (b, h, 0, 0, 0))],
        out_specs=pl.BlockSpec((None, V_HEAD_DIM, Q_TILE), lambda b, h, i: (b, h, i)),
        out_shape=jax.ShapeDtypeStruct((batch, MLA_HEADS * V_HEAD_DIM, seq), BF16),
        compiler_params=_params(3),
        name="attn",
    )(qt, k.reshape(batch, seq, MLA_HEADS * HEAD_PAD), vt)


def _merge_kernel(x_ref, mod_ref, g_ref, f_ref, ot_ref, wg_ref, wfo_ref, wmo_ref, wout_ref, o_ref):
    x = x_ref[...]
    d = x.shape[1]
    h = _norm_mod(x, g_ref[...], mod_ref[3:4, :], mod_ref[4:5, :]).astype(BF16)
    gl = _dot(h, wg_ref[...])
    y_a = _dot(f_ref[...], wfo_ref[...])
    y_b = _dot_tn(ot_ref[...], wmo_ref[...])
    y = jax.nn.sigmoid(gl[:, :d]) * y_a + jax.nn.sigmoid(gl[:, d:]) * y_b
    o_ref[...] = x + mod_ref[5:6, :] * _dot(y.astype(BF16), wout_ref[...])


def _merge(x2d, mod, norm_g, f, ot, w_g, w_fo, w_mo, w_out, *, seq):
    n, d = x2d.shape
    per_batch = seq // ROW_TILE
    hv = MLA_HEADS * V_HEAD_DIM
    return pl.pallas_call(
        _merge_kernel,
        grid=(n // ROW_TILE,),
        in_specs=[pl.BlockSpec((ROW_TILE, d), lambda i: (i, 0)),
                  pl.BlockSpec((None, N_ADA, d), lambda i: (i // per_batch, 0, 0)),
                  _const_spec((1, d)),
                  pl.BlockSpec((ROW_TILE, FOURIER_WIDTH), lambda i: (i, 0)),
                  pl.BlockSpec((None, hv, ROW_TILE), lambda i: (i // per_batch, 0, i % per_batch)),
                  _const_spec(w_g.shape), _const_spec(w_fo.shape), _const_spec(w_mo.shape),
                  _const_spec(w_out.shape)],
        out_specs=pl.BlockSpec((ROW_TILE, d), lambda i: (i, 0)),
        out_shape=jax.ShapeDtypeStruct((n, d), F32),
        compiler_params=_params(1),
        name="merge",
    )(x2d, mod, norm_g.reshape(1, d), f, ot, w_g, w_fo, w_mo, w_out)


def kernel(x, c, positions, ada_w, ada_b, ffn1_norm, ffn1_w_gate, ffn1_w_up, ffn1_w_down, mix_norm,
           w_in, q_norm, w_q_up, kv_norm, w_kv_up, w_fourier_out, w_mla_out, w_out, ffn2_norm,
           ffn2_w_gate, ffn2_w_up, ffn2_w_down, final_norm):
    batch, seq, d = x.shape
    depth = ada_w.shape[0]
    x2d = x.reshape(batch * seq, d)
    pos_rows = positions.astype(F32).reshape(batch * seq // ROW_TILE, 1, ROW_TILE)
    inv_freq = (1.0 / (ROPE_THETA ** (jnp.arange(ROPE_HALF, dtype=F32) * 2.0 / QK_ROPE_DIM)))
    inv_freq = inv_freq.reshape(ROPE_HALF, 1)
    q_scale = float(QK_DIM ** -0.5 * np.log2(np.e))
    cs, w1, tc, ts, c3, s3 = _dft_constants(seq)
    s1 = FOURIER_WIDTH
    s2 = s1 + Q_LORA_RANK
    s3_ = s2 + KV_LORA_RANK
    s4 = s3_ + QK_ROPE_DIM

    for l in range(depth):
        mod = _ada(c, ada_w[l], ada_b[l])
        bf = lambda a: a.astype(BF16)
        x2d = _ffn(x2d, mod, ffn1_norm[l], bf(ffn1_w_gate[l]), bf(ffn1_w_up[l]), bf(ffn1_w_down[l]),
                   final_norm, sub=0, final=False, seq=seq)

        w_in_l = w_in[l]
        w_qt = w_q_up[l].T.reshape(MLA_HEADS, QK_DIM, Q_LORA_RANK).reshape(MLA_HEADS * QK_DIM, -1)
        w_kv = w_kv_up[l].reshape(KV_LORA_RANK, MLA_HEADS, QK_NOPE_DIM + V_HEAD_DIM)
        w_knt = w_kv[:, :, :QK_NOPE_DIM].reshape(KV_LORA_RANK, -1).T
        w_vt = w_kv[:, :, QK_NOPE_DIM:].reshape(KV_LORA_RANK, -1).T
        ab, qt, k, vt = _mix_in(
            x2d, mod, mix_norm[l], pos_rows, inv_freq, bf(w_in_l[:, :s3_]), bf(w_in_l[:, s3_:s4].T),
            q_norm[l], kv_norm[l], bf(w_qt), bf(w_knt), bf(w_vt), cs,
            batch=batch, seq=seq, q_scale=q_scale)
        f = _seq_dft(ab, w1, tc, ts, c3, s3, batch=batch, seq=seq)
        ot = _attention(qt, k, vt, batch=batch, seq=seq)
        x2d = _merge(x2d, mod, mix_norm[l], f, ot, bf(w_in_l[:, s4:]), bf(w_fourier_out[l]),
                     bf(w_mla_out[l]), bf(w_out[l]), seq=seq)

        x2d = _ffn(x2d, mod, ffn2_norm[l], bf(ffn2_w_gate[l]), bf(ffn2_w_up[l]), bf(ffn2_w_down[l]),
                   final_norm, sub=2, final=(l == depth - 1), seq=seq)
    return x2d.reshape(batch, seq, d)
```

```python
import functools

import numpy as np
import jax
import jax.numpy as jnp
from jax import lax
from jax.experimental import pallas as pl
from jax.experimental.pallas import tpu as pltpu

F32 = jnp.float32
BF16 = jnp.bfloat16

FOURIER_GROUPS = 4
FOURIER_GROUP_DIM = 128
FOURIER_WIDTH = FOURIER_GROUPS * FOURIER_GROUP_DIM
MLA_HEADS = 8
QK_NOPE_DIM = 64
QK_ROPE_DIM = 32
QK_DIM = QK_NOPE_DIM + QK_ROPE_DIM
V_HEAD_DIM = 64
Q_LORA_RANK = 384
KV_LORA_RANK = 256
ROPE_THETA = 10000.0
NORM_EPS = 1e-6
N_ADA = 9

HEAD_PAD = 128
ROPE_HALF = QK_ROPE_DIM // 2
FFT_N1 = 128
ROW_TILE = 512
KV_TILE = 256
Q_TILE = 512
ATTN_AHEAD = 2
ATTN_LAG = 2
ATTN_MAX_EXCESS = 60.0
FFT_PAIRS = 8
VMEM_LIMIT = 56 * 1024 * 1024


def _params(n_axes, flags=None):
    return pltpu.CompilerParams(
        dimension_semantics=("parallel",) * n_axes, vmem_limit_bytes=VMEM_LIMIT, flags=flags)


def _const_spec(shape):
    nd = len(shape)
    return pl.BlockSpec(shape, lambda *_: (0,) * nd, pipeline_mode=pl.Buffered(1))


def _rms(x, g):
    ms = jnp.mean(x * x, axis=-1, keepdims=True)
    return x * lax.rsqrt(ms + NORM_EPS) * g


def _norm_mod(x, g, shift, scale):
    return _rms(x, g) * (1.0 + scale) + shift


def _dot(a, b):
    return jnp.dot(a, b, preferred_element_type=F32)


def _dot_nt(a, b):
    return lax.dot_general(a, b, (((1,), (1,)), ((), ())), preferred_element_type=F32)


def _pair_rows(x):
    return pltpu.bitcast(x.astype(BF16), jnp.uint32)


def _dot_tn(a, b):
    return lax.dot_general(a, b, (((0,), (0,)), ((), ())), preferred_element_type=F32)


def _ada_kernel(c_ref, w_ref, b_ref, o_ref):
    c = c_ref[...]
    ca = c * jax.nn.sigmoid(c)
    o_ref[...] = jnp.dot(ca, w_ref[...], preferred_element_type=F32,
                         precision=lax.Precision.HIGHEST) + b_ref[...]


def _ada(c, ada_w, ada_b):
    b, d = c.shape
    rows = 8
    c_pad = jnp.zeros((rows, d), F32).at[:b].set(c)
    out = pl.pallas_call(
        _ada_kernel,
        grid=(N_ADA,),
        in_specs=[pl.BlockSpec((rows, d), lambda j: (0, 0)),
                  pl.BlockSpec((d, d), lambda j: (0, j)),
                  pl.BlockSpec((1, d), lambda j: (0, j))],
        out_specs=pl.BlockSpec((rows, d), lambda j: (0, j)),
        out_shape=jax.ShapeDtypeStruct((rows, N_ADA * d), F32),
        compiler_params=_params(1),
        name="ada",
    )(c_pad, ada_w, ada_b.reshape(1, -1))
    return out[:b].reshape(b, N_ADA, d)


def _ffn_kernel(x_ref, mod_ref, g_ref, wg_ref, wu_ref, wd_ref, fn_ref, o_ref, *, sub, final):
    x = x_ref[...]
    shift = mod_ref[3 * sub:3 * sub + 1, :]
    scale = mod_ref[3 * sub + 1:3 * sub + 2, :]
    gate = mod_ref[3 * sub + 2:3 * sub + 3, :]
    h = _norm_mod(x, g_ref[...], shift, scale).astype(BF16)
    gg = _dot(h, wg_ref[...])
    uu = _dot(h, wu_ref[...])
    a = (gg * jax.nn.sigmoid(gg) * uu).astype(BF16)
    y = x + (0.5 * gate) * _dot(a, wd_ref[...])
    if final:
        y = _rms(y, fn_ref[...])
    o_ref[...] = y


def _ffn(x2d, mod, norm_g, wg, wu, wd, final_g, *, sub, final, seq):
    n, d = x2d.shape
    dff = wg.shape[1]
    per_batch = seq // ROW_TILE
    return pl.pallas_call(
        functools.partial(_ffn_kernel, sub=sub, final=final),
        grid=(n // ROW_TILE,),
        in_specs=[pl.BlockSpec((ROW_TILE, d), lambda i: (i, 0)),
                  pl.BlockSpec((None, N_ADA, d), lambda i: (i // per_batch, 0, 0)),
                  _const_spec((1, d)),
                  _const_spec((d, dff)), _const_spec((d, dff)), _const_spec((dff, d)),
                  _const_spec((1, d))],
        out_specs=pl.BlockSpec((ROW_TILE, d), lambda i: (i, 0)),
        out_shape=jax.ShapeDtypeStruct((n, d), F32),
        compiler_params=_params(1),
        name="ffn%d" % sub,
    )(x2d, mod, norm_g.reshape(1, d), wg, wu, wd, final_g.reshape(1, d))


def _mix_in_kernel(x_ref, mod_ref, g_ref, pos_ref, invf_ref, w_in_ref, w_krt_ref, qn_ref, kvn_ref,
                   w_qt_ref, w_knt_ref, w_vt_ref, cs_ref,
                   ab_ref, qt_ref, k_ref, vt_ref, *, q_scale):
    x = x_ref[...]
    h = _norm_mod(x, g_ref[...], mod_ref[3:4, :], mod_ref[4:5, :]).astype(BF16)
    z = _dot(h, w_in_ref[...])
    tile = x.shape[0]

    cs = cs_ref[...]
    for g in range(FOURIER_GROUPS):
        lo = g * FOURIER_GROUP_DIM
        u = z[:, lo:lo + FOURIER_GROUP_DIM].astype(BF16)
        ab = _pair_rows(_dot(u, cs))
        ab_ref[:, lo:lo + FOURIER_GROUP_DIM] = ab[:, :FOURIER_GROUP_DIM]
        ab_ref[:, FOURIER_WIDTH + lo:FOURIER_WIDTH + lo + FOURIER_GROUP_DIM] = ab[:, FOURIER_GROUP_DIM:]

    ang = invf_ref[...] * pos_ref[...]
    cos_t = jnp.cos(ang)
    sin_t = jnp.sin(ang)

    def rope_t(v):
        v1, v2 = v[:ROPE_HALF], v[ROPE_HALF:]
        return v1 * cos_t - v2 * sin_t, v2 * cos_t + v1 * sin_t

    pad = jnp.zeros((HEAD_PAD - QK_DIM, tile), F32)

    q_lat = z[:, FOURIER_WIDTH:FOURIER_WIDTH + Q_LORA_RANK]
    qn = _rms(q_lat, qn_ref[...]).astype(BF16)
    qt = _dot_nt(w_qt_ref[...], qn)
    for hd in range(MLA_HEADS):
        lo = hd * QK_DIM
        r1, r2 = rope_t(qt[lo + QK_NOPE_DIM:lo + QK_DIM])
        qh = jnp.concatenate([qt[lo:lo + QK_NOPE_DIM], r1, r2], axis=0) * q_scale
        qt_ref[hd * HEAD_PAD:(hd + 1) * HEAD_PAD, :] = (
            jnp.concatenate([qh, pad], axis=0).astype(BF16))

    kv_lat = z[:, FOURIER_WIDTH + Q_LORA_RANK:]
    kvn = _rms(kv_lat, kvn_ref[...]).astype(BF16)
    knt = _dot_nt(w_knt_ref[...], kvn)
    vt = _dot_nt(w_vt_ref[...], kvn)
    kr1, kr2 = rope_t(_dot_nt(w_krt_ref[...], h))
    for hd in range(MLA_HEADS):
        lo = hd * QK_NOPE_DIM
        kt = jnp.concatenate([knt[lo:lo + QK_NOPE_DIM], kr1, kr2, pad], axis=0)
        k_ref[:, hd * HEAD_PAD:(hd + 1) * HEAD_PAD] = kt.T.astype(BF16)
        for cch in range(tile // KV_TILE):
            vt_ref[hd, cch] = vt[hd * V_HEAD_DIM:(hd + 1) * V_HEAD_DIM,
                                 cch * KV_TILE:(cch + 1) * KV_TILE].astype(BF16)


def _mix_in(x2d, mod, norm_g, pos_rows, inv_freq, w_in_a, w_krt, q_norm, kv_norm, w_qt, w_knt, w_vt,
            cs, *, batch, seq, q_scale):
    n, d = x2d.shape
    per_batch = seq // ROW_TILE
    assert ROW_TILE % KV_TILE == 0
    chunks_per_tile = ROW_TILE // KV_TILE
    ispec = [pl.BlockSpec((ROW_TILE, d), lambda i: (i, 0)),
             pl.BlockSpec((None, N_ADA, d), lambda i: (i // per_batch, 0, 0)),
             _const_spec((1, d)),
             pl.BlockSpec((None, 1, ROW_TILE), lambda i: (i, 0, 0)),
             _const_spec(inv_freq.shape),
             _const_spec(w_in_a.shape), _const_spec(w_krt.shape),
             _const_spec((1, Q_LORA_RANK)), _const_spec((1, KV_LORA_RANK)),
             _const_spec(w_qt.shape), _const_spec(w_knt.shape), _const_spec(w_vt.shape),
             _const_spec(cs.shape)]
    hp = MLA_HEADS * HEAD_PAD
    ospec = [pl.BlockSpec((ROW_TILE // 2, 2 * FOURIER_WIDTH), lambda i: (i, 0)),
             pl.BlockSpec((None, hp, ROW_TILE), lambda i: (i // per_batch, 0, i % per_batch)),
             pl.BlockSpec((ROW_TILE, hp), lambda i: (i, 0)),
             pl.BlockSpec((None, MLA_HEADS, chunks_per_tile, V_HEAD_DIM, KV_TILE),
                          lambda i: (i // per_batch, 0, i % per_batch, 0, 0))]
    oshape = [jax.ShapeDtypeStruct((n // 2, 2 * FOURIER_WIDTH), jnp.uint32),
              jax.ShapeDtypeStruct((batch, hp, seq), BF16),
              jax.ShapeDtypeStruct((n, hp), BF16),
              jax.ShapeDtypeStruct((batch, MLA_HEADS, seq // KV_TILE, V_HEAD_DIM, KV_TILE), BF16)]
    return pl.pallas_call(
        functools.partial(_mix_in_kernel, q_scale=q_scale),
        grid=(n // ROW_TILE,),
        in_specs=ispec, out_specs=ospec, out_shape=oshape,
        compiler_params=_params(1),
        name="mix_in",
    )(x2d, mod, norm_g.reshape(1, d), pos_rows, inv_freq, w_in_a, w_krt,
      q_norm.reshape(1, -1), kv_norm.reshape(1, -1), w_qt, w_knt, w_vt, cs)


def _paired_dft(words, w):
    x = pltpu.bitcast(words, BF16)
    k = x.shape[0]
    return _dot(w[:, :k], x[:, :FOURIER_WIDTH]) + _dot(w[:, k:], x[:, FOURIER_WIDTH:])


def _fft_a_kernel(x_ref, w_ref, tc_ref, ts_ref, o_ref):
    w = w_ref[...]
    n2 = w.shape[0] // 4
    for j in range(FFT_PAIRS):
        g = _paired_dft(x_ref[:, j, :], w)
        for par in range(2):
            gr = g[2 * par * n2:(2 * par + 1) * n2]
            gi = g[(2 * par + 1) * n2:(2 * par + 2) * n2]
            tc = jnp.concatenate([tc_ref[2 * j + par]] * FOURIER_GROUPS, axis=1)
            ts = jnp.concatenate([ts_ref[2 * j + par]] * FOURIER_GROUPS, axis=1)
            h = jnp.concatenate([gr * tc - gi * ts, gi * tc + gr * ts], axis=1)
            o_ref[2 * j + par] = _pair_rows(h)


def _fft_b_kernel(x_ref, w_ref, o_ref):
    w = w_ref[...]
    n1 = w.shape[0] // 2
    for j in range(FFT_PAIRS):
        f = _paired_dft(x_ref[:, j, :], w)
        o_ref[2 * j] = f[:n1]
        o_ref[2 * j + 1] = f[n1:]


def _seq_dft(ab_words, wa, tc, ts, wb, *, batch, seq):
    n1, n2 = FFT_N1, seq // FFT_N1
    wd = 2 * FOURIER_WIDTH
    x = ab_words.reshape(batch, n2, n1 // 2, wd)
    rows = 2 * FFT_PAIRS
    y = pl.pallas_call(
        _fft_a_kernel,
        grid=(batch, n1 // rows),
        in_specs=[pl.BlockSpec((None, n2, FFT_PAIRS, wd), lambda b, u: (b, 0, u, 0)),
                  _const_spec(wa.shape),
                  pl.BlockSpec((rows, n2, 128), lambda b, u: (u, 0, 0)),
                  pl.BlockSpec((rows, n2, 128), lambda b, u: (u, 0, 0))],
        out_specs=pl.BlockSpec((None, rows, n2 // 2, wd), lambda b, u: (b, u, 0, 0)),
        out_shape=jax.ShapeDtypeStruct((batch, n1, n2 // 2, wd), jnp.uint32),
        compiler_params=_params(2),
        name="fft_a",
    )(x, wa, tc, ts)
    return pl.pallas_call(
        _fft_b_kernel,
        grid=(batch, n2 // rows),
        in_specs=[pl.BlockSpec((None, n1, FFT_PAIRS, wd), lambda b, v: (b, 0, v, 0)),
                  _const_spec(wb.shape)],
        out_specs=pl.BlockSpec((None, rows, n1, FOURIER_WIDTH), lambda b, v: (b, v, 0, 0)),
        out_shape=jax.ShapeDtypeStruct((batch, n2, n1, FOURIER_WIDTH), F32),
        compiler_params=_params(2),
        name="fft_b",
    )(y, wb)


def _dft_constants(seq):
    n1, n2 = FFT_N1, seq // FFT_N1
    gd = FOURIER_GROUP_DIM
    i = np.arange(gd)
    ang_c = 2.0 * np.pi * np.outer(i, i) / gd
    cs = np.concatenate([np.cos(ang_c), np.sin(ang_c)], axis=1)
    i2 = np.arange(n2)
    ang_a = 2.0 * np.pi * np.outer(i2, i2) / n2
    ca, sa = np.cos(ang_a), np.sin(ang_a)
    i1 = np.arange(n1)
    ang_t = 2.0 * np.pi * np.outer(i1, i2) / seq
    ang_b = 2.0 * np.pi * np.outer(i1, i1) / n1
    norm = 1.0 / np.sqrt(float(seq) * gd)
    cb, sb = np.cos(ang_b) * norm, np.sin(ang_b) * norm

    def spread(re_part, im_part):
        m, r = re_part.shape
        out = np.zeros((2, m, 2, 2 * r))
        for p in range(2):
            out[p, :, 0, p::2] = re_part
            out[p, :, 1, p::2] = im_part
        return out.reshape(2 * m, 4 * r)

    wa = spread(np.concatenate([ca, sa], axis=0), np.concatenate([-sa, ca], axis=0))
    wb = spread(cb, -sb)
    f32 = lambda a: jnp.asarray(np.asarray(a, np.float32))
    tc = jnp.broadcast_to(f32(np.cos(ang_t))[:, :, None], (n1, n2, 128))
    ts = jnp.broadcast_to(f32(np.sin(ang_t))[:, :, None], (n1, n2, 128))
    return f32(cs).astype(BF16), f32(wa).astype(BF16), tc, ts, f32(wb).astype(BF16)


def _attn_kernel(qt_ref, k_ref, vt_ref, o_ref):
    qt = qt_ref[...]
    tq = qt.shape[1]
    n_chunks = vt_ref.shape[0]
    ones = jnp.ones((16, KV_TILE), BF16)

    def qk(j):
        return _dot(k_ref[j * KV_TILE:(j + 1) * KV_TILE, :], qt)

    def pv(j, p):
        return _dot(jnp.concatenate([vt_ref[j], ones], axis=0), p)

    s0 = qk(0)
    m = jnp.max(s0, axis=0, keepdims=True)
    inflight = [qk(j) for j in range(1, 1 + ATTN_AHEAD)]
    acc = pv(0, jnp.exp2(s0 - m).astype(BF16))
    excess = jnp.zeros((1, tq), F32)
    pending = []
    for j in range(1, n_chunks):
        s = inflight.pop(0)
        if j + ATTN_AHEAD < n_chunks:
            inflight.append(qk(j + ATTN_AHEAD))
        cm = jnp.max(s, axis=0, keepdims=True)
        acc = acc + pv(j, jnp.exp2(s - m).astype(BF16))
        excess = jnp.maximum(excess, cm - m)
        pending.append(cm)
        if len(pending) >= ATTN_LAG:
            m_new = jnp.maximum(m, pending.pop(0))
            acc = acc * jnp.exp2(m - m_new)
            m = m_new
    o_ref[...] = (acc[:V_HEAD_DIM] / acc[V_HEAD_DIM:V_HEAD_DIM + 1]).astype(BF16)

    @pl.when(jnp.max(excess) > ATTN_MAX_EXCESS)
    def _():
        def body(j, carry):
            m_c, acc_c = carry
            kc = k_ref[pl.ds(pl.multiple_of(j * KV_TILE, KV_TILE), KV_TILE), :]
            s_c = _dot(kc, qt)
            m_n = jnp.maximum(m_c, jnp.max(s_c, axis=0, keepdims=True))
            p_c = jnp.exp2(s_c - m_n).astype(BF16)
            return m_n, jnp.exp2(m_c - m_n) * acc_c + pv(j, p_c)

        m_0 = jnp.full((1, tq), -jnp.inf, F32)
        acc_0 = jnp.zeros((V_HEAD_DIM + 16, tq), F32)
        _, acc_s = lax.fori_loop(0, n_chunks, body, (m_0, acc_0))
        o_ref[...] = (acc_s[:V_HEAD_DIM] / acc_s[V_HEAD_DIM:V_HEAD_DIM + 1]).astype(BF16)


def _attention(qt, k, vt, *, batch, seq):
    n_chunks = seq // KV_TILE
    return pl.pallas_call(
        _attn_kernel,
        grid=(batch, MLA_HEADS, seq // Q_TILE),
        in_specs=[pl.BlockSpec((None, HEAD_PAD, Q_TILE), lambda b, h, i: (b, h, i)),
                  pl.BlockSpec((None, seq, HEAD_PAD), lambda b, h, i: (b, 0, h)),
                  pl.BlockSpec((None, None, n_chunks, V_HEAD_DIM, KV_TILE),
                               lambda b, h, i: (b, h, 0, 0, 0))],
        out_specs=pl.BlockSpec((None, V_HEAD_DIM, Q_TILE), lambda b, h, i: (b, h, i)),
        out_shape=jax.ShapeDtypeStruct((batch, MLA_HEADS * V_HEAD_DIM, seq), BF16),
        compiler_params=_params(3),
        name="attn",
    )(qt, k.reshape(batch, seq, MLA_HEADS * HEAD_PAD), vt)


def _merge_kernel(x_ref, mod_ref, g_ref, f_ref, ot_ref, wg_ref, wfo_ref, wmo_ref, wout_ref, o_ref):
    x = x_ref[...]
    d = x.shape[1]
    h = _norm_mod(x, g_ref[...], mod_ref[3:4, :], mod_ref[4:5, :]).astype(BF16)
    gl = _dot(h, wg_ref[...])
    f = jnp.concatenate([f_ref[:, j, :] for j in range(f_ref.shape[1])], axis=0).astype(BF16)
    y_a = _dot(f, wfo_ref[...])
    y_b = _dot_tn(ot_ref[...], wmo_ref[...])
    y = jax.nn.sigmoid(gl[:, :d]) * y_a + jax.nn.sigmoid(gl[:, d:]) * y_b
    o_ref[...] = x + mod_ref[5:6, :] * _dot(y.astype(BF16), wout_ref[...])


def _merge(x2d, mod, norm_g, f, ot, w_g, w_fo, w_mo, w_out, *, seq):
    n, d = x2d.shape
    per_batch = seq // ROW_TILE
    hv = MLA_HEADS * V_HEAD_DIM
    n2 = f.shape[1]
    assert ROW_TILE == FFT_PAIRS * n2
    return pl.pallas_call(
        _merge_kernel,
        grid=(n // ROW_TILE,),
        in_specs=[pl.BlockSpec((ROW_TILE, d), lambda i: (i, 0)),
                  pl.BlockSpec((None, N_ADA, d), lambda i: (i // per_batch, 0, 0)),
                  _const_spec((1, d)),
                  pl.BlockSpec((None, n2, ROW_TILE // n2, FOURIER_WIDTH),
                               lambda i: (i // per_batch, 0, i % per_batch, 0)),
                  pl.BlockSpec((None, hv, ROW_TILE), lambda i: (i // per_batch, 0, i % per_batch)),
                  _const_spec(w_g.shape), _const_spec(w_fo.shape), _const_spec(w_mo.shape),
                  _const_spec(w_out.shape)],
        out_specs=pl.BlockSpec((ROW_TILE, d), lambda i: (i, 0)),
        out_shape=jax.ShapeDtypeStruct((n, d), F32),
        compiler_params=_params(1),
        name="merge",
    )(x2d, mod, norm_g.reshape(1, d), f, ot, w_g, w_fo, w_mo, w_out)


def kernel(x, c, positions, ada_w, ada_b, ffn1_norm, ffn1_w_gate, ffn1_w_up, ffn1_w_down, mix_norm,
           w_in, q_norm, w_q_up, kv_norm, w_kv_up, w_fourier_out, w_mla_out, w_out, ffn2_norm,
           ffn2_w_gate, ffn2_w_up, ffn2_w_down, final_norm):
    batch, seq, d = x.shape
    depth = ada_w.shape[0]
    x2d = x.reshape(batch * seq, d)
    pos_rows = positions.astype(F32).reshape(batch * seq // ROW_TILE, 1, ROW_TILE)
    inv_freq = (1.0 / (ROPE_THETA ** (jnp.arange(ROPE_HALF, dtype=F32) * 2.0 / QK_ROPE_DIM)))
    inv_freq = inv_freq.reshape(ROPE_HALF, 1)
    q_scale = float(QK_DIM ** -0.5 * np.log2(np.e))
    cs, wa, tc, ts, wb = _dft_constants(seq)
    s1 = FOURIER_WIDTH
    s2 = s1 + Q_LORA_RANK
    s3_ = s2 + KV_LORA_RANK
    s4 = s3_ + QK_ROPE_DIM

    for l in range(depth):
        mod = _ada(c, ada_w[l], ada_b[l])
        bf = lambda a: a.astype(BF16)
        x2d = _ffn(x2d, mod, ffn1_norm[l], bf(ffn1_w_gate[l]), bf(ffn1_w_up[l]), bf(ffn1_w_down[l]),
                   final_norm, sub=0, final=False, seq=seq)

        w_in_l = w_in[l]
        w_qt = w_q_up[l].T.reshape(MLA_HEADS, QK_DIM, Q_LORA_RANK).reshape(MLA_HEADS * QK_DIM, -1)
        w_kv = w_kv_up[l].reshape(KV_LORA_RANK, MLA_HEADS, QK_NOPE_DIM + V_HEAD_DIM)
        w_knt = w_kv[:, :, :QK_NOPE_DIM].reshape(KV_LORA_RANK, -1).T
        w_vt = w_kv[:, :, QK_NOPE_DIM:].reshape(KV_LORA_RANK, -1).T
        ab, qt, k, vt = _mix_in(
            x2d, mod, mix_norm[l], pos_rows, inv_freq, bf(w_in_l[:, :s3_]), bf(w_in_l[:, s3_:s4].T),
            q_norm[l], kv_norm[l], bf(w_qt), bf(w_knt), bf(w_vt), cs,
            batch=batch, seq=seq, q_scale=q_scale)
        f = _seq_dft(ab, wa, tc, ts, wb, batch=batch, seq=seq)
        ot = _attention(qt, k, vt, batch=batch, seq=seq)
        x2d = _merge(x2d, mod, mix_norm[l], f, ot, bf(w_in_l[:, s4:]), bf(w_fourier_out[l]),
                     bf(w_mla_out[l]), bf(w_out[l]), seq=seq)

        x2d = _ffn(x2d, mod, ffn2_norm[l], bf(ffn2_w_gate[l]), bf(ffn2_w_up[l]), bf(ffn2_w_down[l]),
                   final_norm, sub=2, final=(l == depth - 1), seq=seq)
    return x2d.reshape(batch, seq, d)
```

```python
import functools

import numpy as np
import jax
import jax.numpy as jnp
from jax import lax
from jax.experimental import pallas as pl
from jax.experimental.pallas import tpu as pltpu

F32 = jnp.float32
BF16 = jnp.bfloat16

FOURIER_GROUPS = 4
FOURIER_GROUP_DIM = 128
FOURIER_WIDTH = FOURIER_GROUPS * FOURIER_GROUP_DIM
MLA_HEADS = 8
QK_NOPE_DIM = 64
QK_ROPE_DIM = 32
QK_DIM = QK_NOPE_DIM + QK_ROPE_DIM
V_HEAD_DIM = 64
Q_LORA_RANK = 384
KV_LORA_RANK = 256
ROPE_THETA = 10000.0
NORM_EPS = 1e-6
N_ADA = 9

HEAD_PAD = 128
ROPE_HALF = QK_ROPE_DIM // 2
FFT_N1 = 128
ROW_TILE = 512
MIX_SLABS = 1
FFN_TILE = 1024
ROW_SLABS = 2
KV_TILE = 256
Q_TILE = 512
ATTN_Q_UNROLL = 2
ATTN_AHEAD = 2
ATTN_LAG = 2
ATTN_MAX_EXCESS = 60.0
FFT_PAIRS = 8
VMEM_LIMIT = 56 * 1024 * 1024


def _params(n_axes, flags=None):
    return pltpu.CompilerParams(
        dimension_semantics=("parallel",) * n_axes, vmem_limit_bytes=VMEM_LIMIT, flags=flags)


def _const_spec(shape):
    nd = len(shape)
    return pl.BlockSpec(shape, lambda *_: (0,) * nd, pipeline_mode=pl.Buffered(1))


def _rms(x, g):
    ms = jnp.mean(x * x, axis=-1, keepdims=True)
    return x * lax.rsqrt(ms + NORM_EPS) * g


def _norm_mod(x, g, shift, scale):
    return _rms(x, g) * (1.0 + scale) + shift


def _dot(a, b):
    return jnp.dot(a, b, preferred_element_type=F32)


def _dot_nt(a, b):
    return lax.dot_general(a, b, (((1,), (1,)), ((), ())), preferred_element_type=F32)


def _pair_rows(x):
    return pltpu.bitcast(x.astype(BF16), jnp.uint32)


def _dot_tn(a, b):
    return lax.dot_general(a, b, (((0,), (0,)), ((), ())), preferred_element_type=F32)


def _ada_kernel(c_ref, w_ref, b_ref, o_ref):
    c = c_ref[...]
    ca = c * jax.nn.sigmoid(c)
    o_ref[...] = jnp.dot(ca, w_ref[...], preferred_element_type=F32,
                         precision=lax.Precision.HIGHEST) + b_ref[...]


def _ada(c, ada_w, ada_b):
    b, d = c.shape
    rows = 8
    c_pad = jnp.zeros((rows, d), F32).at[:b].set(c)
    out = pl.pallas_call(
        _ada_kernel,
        grid=(N_ADA,),
        in_specs=[pl.BlockSpec((rows, d), lambda j: (0, 0)),
                  pl.BlockSpec((d, d), lambda j: (0, j)),
                  pl.BlockSpec((1, d), lambda j: (0, j))],
        out_specs=pl.BlockSpec((rows, d), lambda j: (0, j)),
        out_shape=jax.ShapeDtypeStruct((rows, N_ADA * d), F32),
        compiler_params=_params(1),
        name="ada",
    )(c_pad, ada_w, ada_b.reshape(1, -1))
    return out[:b].reshape(b, N_ADA, d)


def _ffn_kernel(x_ref, mod_ref, g_ref, wg_ref, wu_ref, wd_ref, fn_ref, o_ref, *, sub, final):
    shift = mod_ref[3 * sub:3 * sub + 1, :]
    scale = mod_ref[3 * sub + 1:3 * sub + 2, :]
    gate = mod_ref[3 * sub + 2:3 * sub + 3, :]
    slab = x_ref.shape[0] // ROW_SLABS
    for r in range(ROW_SLABS):
        rows = slice(r * slab, (r + 1) * slab)
        x = x_ref[rows, :]
        h = _norm_mod(x, g_ref[...], shift, scale).astype(BF16)
        gg = _dot(h, wg_ref[...])
        uu = _dot(h, wu_ref[...])
        a = (gg * jax.nn.sigmoid(gg) * uu).astype(BF16)
        y = x + (0.5 * gate) * _dot(a, wd_ref[...])
        if final:
            y = _rms(y, fn_ref[...])
        o_ref[rows, :] = y


def _ffn(x2d, mod, norm_g, wg, wu, wd, final_g, *, sub, final, seq):
    n, d = x2d.shape
    dff = wg.shape[1]
    per_batch = seq // FFN_TILE
    return pl.pallas_call(
        functools.partial(_ffn_kernel, sub=sub, final=final),
        grid=(n // FFN_TILE,),
        in_specs=[pl.BlockSpec((FFN_TILE, d), lambda i: (i, 0)),
                  pl.BlockSpec((None, N_ADA, d), lambda i: (i // per_batch, 0, 0)),
                  _const_spec((1, d)),
                  _const_spec((d, dff)), _const_spec((d, dff)), _const_spec((dff, d)),
                  _const_spec((1, d))],
        out_specs=pl.BlockSpec((FFN_TILE, d), lambda i: (i, 0)),
        out_shape=jax.ShapeDtypeStruct((n, d), F32),
        compiler_params=_params(1),
        name="ffn%d" % sub,
    )(x2d, mod, norm_g.reshape(1, d), wg, wu, wd, final_g.reshape(1, d))


def _mix_in_kernel(x_ref, mod_ref, g_ref, pos_ref, invf_ref, w_in_ref, w_krt_ref, qn_ref, kvn_ref,
                   w_qt_ref, w_knt_ref, w_vt_ref, cs_ref,
                   ab_ref, qt_ref, k_ref, vt_ref, *, q_scale):
    slab = x_ref.shape[0] // MIX_SLABS
    for r in range(MIX_SLABS):
        _mix_in_slab(r * slab, slab, x_ref, mod_ref, g_ref, pos_ref, invf_ref, w_in_ref, w_krt_ref,
                     qn_ref, kvn_ref, w_qt_ref, w_knt_ref, w_vt_ref, cs_ref,
                     ab_ref, qt_ref, k_ref, vt_ref, q_scale)


def _mix_in_slab(r0, tile, x_ref, mod_ref, g_ref, pos_ref, invf_ref, w_in_ref, w_krt_ref, qn_ref, kvn_ref,
                 w_qt_ref, w_knt_ref, w_vt_ref, cs_ref, ab_ref, qt_ref, k_ref, vt_ref, q_scale):
    rows = slice(r0, r0 + tile)
    x = x_ref[rows, :]
    h = _norm_mod(x, g_ref[...], mod_ref[3:4, :], mod_ref[4:5, :]).astype(BF16)
    z = _dot(h, w_in_ref[...])

    cs = cs_ref[...]
    pair_rows = slice(r0 // 2, (r0 + tile) // 2)
    for g in range(FOURIER_GROUPS):
        lo = g * FOURIER_GROUP_DIM
        u = z[:, lo:lo + FOURIER_GROUP_DIM].astype(BF16)
        ab = _pair_rows(_dot(u, cs))
        ab_ref[pair_rows, lo:lo + FOURIER_GROUP_DIM] = ab[:, :FOURIER_GROUP_DIM]
        ab_ref[pair_rows, FOURIER_WIDTH + lo:FOURIER_WIDTH + lo + FOURIER_GROUP_DIM] = (
            ab[:, FOURIER_GROUP_DIM:])

    ang = invf_ref[...] * pos_ref[:, rows]
    cos_t = jnp.cos(ang)
    sin_t = jnp.sin(ang)

    def rope_t(v):
        v1, v2 = v[:ROPE_HALF], v[ROPE_HALF:]
        return v1 * cos_t - v2 * sin_t, v2 * cos_t + v1 * sin_t

    pad = jnp.zeros((HEAD_PAD - QK_DIM, tile), F32)

    q_lat = z[:, FOURIER_WIDTH:FOURIER_WIDTH + Q_LORA_RANK]
    qn = _rms(q_lat, qn_ref[...]).astype(BF16)
    qt = _dot_nt(w_qt_ref[...], qn)
    for hd in range(MLA_HEADS):
        lo = hd * QK_DIM
        r1, r2 = rope_t(qt[lo + QK_NOPE_DIM:lo + QK_DIM])
        qh = jnp.concatenate([qt[lo:lo + QK_NOPE_DIM], r1, r2], axis=0) * q_scale
        qt_ref[hd, :, rows] = jnp.concatenate([qh, pad], axis=0).astype(BF16)

    kv_lat = z[:, FOURIER_WIDTH + Q_LORA_RANK:]
    kvn = _rms(kv_lat, kvn_ref[...]).astype(BF16)
    knt = _dot_nt(w_knt_ref[...], kvn)
    vt = _dot_nt(w_vt_ref[...], kvn)
    kr1, kr2 = rope_t(_dot_nt(w_krt_ref[...], h))
    for hd in range(MLA_HEADS):
        lo = hd * QK_NOPE_DIM
        kt = jnp.concatenate([knt[lo:lo + QK_NOPE_DIM], kr1, kr2, pad], axis=0)
        k_ref[rows, hd * HEAD_PAD:(hd + 1) * HEAD_PAD] = kt.T.astype(BF16)
        for cch in range(tile // KV_TILE):
            vt_ref[hd, r0 // KV_TILE + cch] = vt[hd * V_HEAD_DIM:(hd + 1) * V_HEAD_DIM,
                                                 cch * KV_TILE:(cch + 1) * KV_TILE].astype(BF16)


def _mix_in(x2d, mod, norm_g, pos_rows, inv_freq, w_in_a, w_krt, q_norm, kv_norm, w_qt, w_knt, w_vt,
            cs, *, batch, seq, q_scale):
    n, d = x2d.shape
    per_batch = seq // ROW_TILE
    assert ROW_TILE % KV_TILE == 0
    chunks_per_tile = ROW_TILE // KV_TILE
    ispec = [pl.BlockSpec((ROW_TILE, d), lambda i: (i, 0)),
             pl.BlockSpec((None, N_ADA, d), lambda i: (i // per_batch, 0, 0)),
             _const_spec((1, d)),
             pl.BlockSpec((None, 1, ROW_TILE), lambda i: (i, 0, 0)),
             _const_spec(inv_freq.shape),
             _const_spec(w_in_a.shape), _const_spec(w_krt.shape),
             _const_spec((1, Q_LORA_RANK)), _const_spec((1, KV_LORA_RANK)),
             _const_spec(w_qt.shape), _const_spec(w_knt.shape), _const_spec(w_vt.shape),
             _const_spec(cs.shape)]
    hp = MLA_HEADS * HEAD_PAD
    ospec = [pl.BlockSpec((ROW_TILE // 2, 2 * FOURIER_WIDTH), lambda i: (i, 0)),
             pl.BlockSpec((None, MLA_HEADS, None, HEAD_PAD, Q_TILE),
                          lambda i: (i // per_batch, 0, i % per_batch, 0, 0)),
             pl.BlockSpec((ROW_TILE, hp), lambda i: (i, 0)),
             pl.BlockSpec((None, MLA_HEADS, chunks_per_tile, V_HEAD_DIM, KV_TILE),
                          lambda i: (i // per_batch, 0, i % per_batch, 0, 0))]
    oshape = [jax.ShapeDtypeStruct((n // 2, 2 * FOURIER_WIDTH), jnp.uint32),
              jax.ShapeDtypeStruct((batch, MLA_HEADS, seq // Q_TILE, HEAD_PAD, Q_TILE), BF16),
              jax.ShapeDtypeStruct((n, hp), BF16),
              jax.ShapeDtypeStruct((batch, MLA_HEADS, seq // KV_TILE, V_HEAD_DIM, KV_TILE), BF16)]
    return pl.pallas_call(
        functools.partial(_mix_in_kernel, q_scale=q_scale),
        grid=(n // ROW_TILE,),
        in_specs=ispec, out_specs=ospec, out_shape=oshape,
        compiler_params=_params(1),
        name="mix_in",
    )(x2d, mod, norm_g.reshape(1, d), pos_rows, inv_freq, w_in_a, w_krt,
      q_norm.reshape(1, -1), kv_norm.reshape(1, -1), w_qt, w_knt, w_vt, cs)


def _paired_dft(words, w):
    x = pltpu.bitcast(words, BF16)
    k = x.shape[0]
    return _dot(w[:, :k], x[:, :FOURIER_WIDTH]) + _dot(w[:, k:], x[:, FOURIER_WIDTH:])


def _fft_a_kernel(x_ref, w_ref, tc_ref, ts_ref, o_ref):
    w = w_ref[...]
    n2 = w.shape[0] // 4
    for j in range(FFT_PAIRS):
        g = _paired_dft(x_ref[:, j, :], w)
        for par in range(2):
            gr = g[2 * par * n2:(2 * par + 1) * n2]
            gi = g[(2 * par + 1) * n2:(2 * par + 2) * n2]
            tc = jnp.concatenate([tc_ref[2 * j + par]] * FOURIER_GROUPS, axis=1)
            ts = jnp.concatenate([ts_ref[2 * j + par]] * FOURIER_GROUPS, axis=1)
            h = jnp.concatenate([gr * tc - gi * ts, gi * tc + gr * ts], axis=1)
            o_ref[2 * j + par] = _pair_rows(h)


def _fft_b_kernel(x_ref, w_ref, o_ref):
    w = w_ref[...]
    n1 = w.shape[0] // 2
    for j in range(FFT_PAIRS):
        f = _paired_dft(x_ref[:, j, :], w)
        o_ref[2 * j] = f[:n1]
        o_ref[2 * j + 1] = f[n1:]


def _seq_dft(ab_words, wa, tc, ts, wb, *, batch, seq):
    n1, n2 = FFT_N1, seq // FFT_N1
    wd = 2 * FOURIER_WIDTH
    x = ab_words.reshape(batch, n2, n1 // 2, wd)
    rows = 2 * FFT_PAIRS
    y = pl.pallas_call(
        _fft_a_kernel,
        grid=(batch, n1 // rows),
        in_specs=[pl.BlockSpec((None, n2, FFT_PAIRS, wd), lambda b, u: (b, 0, u, 0)),
                  _const_spec(wa.shape),
                  pl.BlockSpec((rows, n2, 128), lambda b, u: (u, 0, 0)),
                  pl.BlockSpec((rows, n2, 128), lambda b, u: (u, 0, 0))],
        out_specs=pl.BlockSpec((None, rows, n2 // 2, wd), lambda b, u: (b, u, 0, 0)),
        out_shape=jax.ShapeDtypeStruct((batch, n1, n2 // 2, wd), jnp.uint32),
        compiler_params=_params(2),
        name="fft_a",
    )(x, wa, tc, ts)
    return pl.pallas_call(
        _fft_b_kernel,
        grid=(batch, n2 // rows),
        in_specs=[pl.BlockSpec((None, n1, FFT_PAIRS, wd), lambda b, v: (b, 0, v, 0)),
                  _const_spec(wb.shape)],
        out_specs=pl.BlockSpec((None, rows, n1, FOURIER_WIDTH), lambda b, v: (b, v, 0, 0)),
        out_shape=jax.ShapeDtypeStruct((batch, n2, n1, FOURIER_WIDTH), F32),
        compiler_params=_params(2),
        name="fft_b",
    )(y, wb)


def _dft_constants(seq):
    n1, n2 = FFT_N1, seq // FFT_N1
    gd = FOURIER_GROUP_DIM
    i = np.arange(gd)
    ang_c = 2.0 * np.pi * np.outer(i, i) / gd
    cs = np.concatenate([np.cos(ang_c), np.sin(ang_c)], axis=1)
    i2 = np.arange(n2)
    ang_a = 2.0 * np.pi * np.outer(i2, i2) / n2
    ca, sa = np.cos(ang_a), np.sin(ang_a)
    i1 = np.arange(n1)
    ang_t = 2.0 * np.pi * np.outer(i1, i2) / seq
    ang_b = 2.0 * np.pi * np.outer(i1, i1) / n1
    norm = 1.0 / np.sqrt(float(seq) * gd)
    cb, sb = np.cos(ang_b) * norm, np.sin(ang_b) * norm

    def spread(re_part, im_part):
        m, r = re_part.shape
        out = np.zeros((2, m, 2, 2 * r))
        for p in range(2):
            out[p, :, 0, p::2] = re_part
            out[p, :, 1, p::2] = im_part
        return out.reshape(2 * m, 4 * r)

    wa = spread(np.concatenate([ca, sa], axis=0), np.concatenate([-sa, ca], axis=0))
    wb = spread(cb, -sb)
    f32 = lambda a: jnp.asarray(np.asarray(a, np.float32))
    tc = jnp.broadcast_to(f32(np.cos(ang_t))[:, :, None], (n1, n2, 128))
    ts = jnp.broadcast_to(f32(np.sin(ang_t))[:, :, None], (n1, n2, 128))
    return f32(cs).astype(BF16), f32(wa).astype(BF16), tc, ts, f32(wb).astype(BF16)


def _attn_kernel(qt_ref, k_ref, vt_ref, o_ref):
    n_q, _, tq = qt_ref.shape
    n_chunks = vt_ref.shape[0]
    ones = jnp.ones((16, KV_TILE), BF16)

    def pv(j, p):
        return _dot(jnp.concatenate([vt_ref[j], ones], axis=0), p)

    def finish(acc):
        return (acc[:V_HEAD_DIM] / acc[V_HEAD_DIM:V_HEAD_DIM + 1]).astype(BF16)

    def streaming_tile(t):
        qt = qt_ref[t]

        def qk(j):
            return _dot(k_ref[j * KV_TILE:(j + 1) * KV_TILE, :], qt)

        s0 = qk(0)
        m = jnp.max(s0, axis=0, keepdims=True)
        inflight = [qk(j) for j in range(1, 1 + ATTN_AHEAD)]
        acc = pv(0, jnp.exp2(s0 - m).astype(BF16))
        excess = jnp.zeros((1, tq), F32)
        pending = []
        for j in range(1, n_chunks):
            s = inflight.pop(0)
            if j + ATTN_AHEAD < n_chunks:
                inflight.append(qk(j + ATTN_AHEAD))
            cm = jnp.max(s, axis=0, keepdims=True)
            acc = acc + pv(j, jnp.exp2(s - m).astype(BF16))
            excess = jnp.maximum(excess, cm - m)
            pending.append(cm)
            if len(pending) >= ATTN_LAG:
                m_new = jnp.maximum(m, pending.pop(0))
                acc = acc * jnp.exp2(m - m_new)
                m = m_new
        o_ref[t] = finish(acc)
        return excess

    def two_pass_tile(t):
        qt = qt_ref[t]

        def body(j, carry):
            m_c, acc_c = carry
            kc = k_ref[pl.ds(pl.multiple_of(j * KV_TILE, KV_TILE), KV_TILE), :]
            s_c = _dot(kc, qt)
            m_n = jnp.maximum(m_c, jnp.max(s_c, axis=0, keepdims=True))
            p_c = jnp.exp2(s_c - m_n).astype(BF16)
            return m_n, jnp.exp2(m_c - m_n) * acc_c + pv(j, p_c)

        m_0 = jnp.full((1, tq), -jnp.inf, F32)
        acc_0 = jnp.zeros((V_HEAD_DIM + 16, tq), F32)
        _, acc_s = lax.fori_loop(0, n_chunks, body, (m_0, acc_0))
        o_ref[t] = finish(acc_s)

    def trip(i, carry):
        tiles = [i * ATTN_Q_UNROLL + u for u in range(ATTN_Q_UNROLL)]
        excess = [streaming_tile(t) for t in tiles]
        for t, e in zip(tiles, excess):
            pl.when(jnp.max(e) > ATTN_MAX_EXCESS)(functools.partial(two_pass_tile, t))
        return carry

    lax.fori_loop(0, n_q // ATTN_Q_UNROLL, trip, 0)


def _attention(qt, k, vt, *, batch, seq):
    n_chunks = seq // KV_TILE
    n_q = seq // Q_TILE
    assert n_q % ATTN_Q_UNROLL == 0
    return pl.pallas_call(
        _attn_kernel,
        grid=(batch, MLA_HEADS),
        in_specs=[pl.BlockSpec((None, None, n_q, HEAD_PAD, Q_TILE), lambda b, h: (b, h, 0, 0, 0)),
                  pl.BlockSpec((None, seq, HEAD_PAD), lambda b, h: (b, 0, h)),
                  pl.BlockSpec((None, None, n_chunks, V_HEAD_DIM, KV_TILE),
                               lambda b, h: (b, h, 0, 0, 0))],
        out_specs=pl.BlockSpec((None, None, n_q, V_HEAD_DIM, Q_TILE), lambda b, h: (b, h, 0, 0, 0)),
        out_shape=jax.ShapeDtypeStruct((batch, MLA_HEADS, n_q, V_HEAD_DIM, Q_TILE), BF16),
        compiler_params=_params(2),
        name="attn",
    )(qt, k.reshape(batch, seq, MLA_HEADS * HEAD_PAD), vt)


def _merge_kernel(x_ref, mod_ref, g_ref, f_ref, ot_ref, wg_ref, wfo_ref, wmo_ref, wout_ref, o_ref):
    x = x_ref[...]
    d = x.shape[1]
    h = _norm_mod(x, g_ref[...], mod_ref[3:4, :], mod_ref[4:5, :]).astype(BF16)
    gl = _dot(h, wg_ref[...])
    f = jnp.concatenate([f_ref[:, j, :] for j in range(f_ref.shape[1])], axis=0).astype(BF16)
    y_a = _dot(f, wfo_ref[...])
    ot = ot_ref[...]
    y_b = _dot_tn(ot.reshape(ot.shape[0] * ot.shape[1], ot.shape[2]), wmo_ref[...])
    y = jax.nn.sigmoid(gl[:, :d]) * y_a + jax.nn.sigmoid(gl[:, d:]) * y_b
    o_ref[...] = x + mod_ref[5:6, :] * _dot(y.astype(BF16), wout_ref[...])


def _merge(x2d, mod, norm_g, f, ot, w_g, w_fo, w_mo, w_out, *, seq):
    n, d = x2d.shape
    per_batch = seq // ROW_TILE
    n2 = f.shape[1]
    assert ROW_TILE == FFT_PAIRS * n2 and ROW_TILE == Q_TILE
    return pl.pallas_call(
        _merge_kernel,
        grid=(n // ROW_TILE,),
        in_specs=[pl.BlockSpec((ROW_TILE, d), lambda i: (i, 0)),
                  pl.BlockSpec((None, N_ADA, d), lambda i: (i // per_batch, 0, 0)),
                  _const_spec((1, d)),
                  pl.BlockSpec((None, n2, ROW_TILE // n2, FOURIER_WIDTH),
                               lambda i: (i // per_batch, 0, i % per_batch, 0)),
                  pl.BlockSpec((None, MLA_HEADS, None, V_HEAD_DIM, Q_TILE),
                               lambda i: (i // per_batch, 0, i % per_batch, 0, 0)),
                  _const_spec(w_g.shape), _const_spec(w_fo.shape), _const_spec(w_mo.shape),
                  _const_spec(w_out.shape)],
        out_specs=pl.BlockSpec((ROW_TILE, d), lambda i: (i, 0)),
        out_shape=jax.ShapeDtypeStruct((n, d), F32),
        compiler_params=_params(1),
        name="merge",
    )(x2d, mod, norm_g.reshape(1, d), f, ot, w_g, w_fo, w_mo, w_out)


def kernel(x, c, positions, ada_w, ada_b, ffn1_norm, ffn1_w_gate, ffn1_w_up, ffn1_w_down, mix_norm,
           w_in, q_norm, w_q_up, kv_norm, w_kv_up, w_fourier_out, w_mla_out, w_out, ffn2_norm,
           ffn2_w_gate, ffn2_w_up, ffn2_w_down, final_norm):
    batch, seq, d = x.shape
    depth = ada_w.shape[0]
    x2d = x.reshape(batch * seq, d)
    pos_rows = positions.astype(F32).reshape(batch * seq // ROW_TILE, 1, ROW_TILE)
    inv_freq = (1.0 / (ROPE_THETA ** (jnp.arange(ROPE_HALF, dtype=F32) * 2.0 / QK_ROPE_DIM)))
    inv_freq = inv_freq.reshape(ROPE_HALF, 1)
    q_scale = float(QK_DIM ** -0.5 * np.log2(np.e))
    cs, wa, tc, ts, wb = _dft_constants(seq)
    s1 = FOURIER_WIDTH
    s2 = s1 + Q_LORA_RANK
    s3_ = s2 + KV_LORA_RANK
    s4 = s3_ + QK_ROPE_DIM

    for l in range(depth):
        mod = _ada(c, ada_w[l], ada_b[l])
        bf = lambda a: a.astype(BF16)
        x2d = _ffn(x2d, mod, ffn1_norm[l], bf(ffn1_w_gate[l]), bf(ffn1_w_up[l]), bf(ffn1_w_down[l]),
                   final_norm, sub=0, final=False, seq=seq)

        w_in_l = w_in[l]
        w_qt = w_q_up[l].T.reshape(MLA_HEADS, QK_DIM, Q_LORA_RANK).reshape(MLA_HEADS * QK_DIM, -1)
        w_kv = w_kv_up[l].reshape(KV_LORA_RANK, MLA_HEADS, QK_NOPE_DIM + V_HEAD_DIM)
        w_knt = w_kv[:, :, :QK_NOPE_DIM].reshape(KV_LORA_RANK, -1).T
        w_vt = w_kv[:, :, QK_NOPE_DIM:].reshape(KV_LORA_RANK, -1).T
        ab, qt, k, vt = _mix_in(
            x2d, mod, mix_norm[l], pos_rows, inv_freq, bf(w_in_l[:, :s3_]), bf(w_in_l[:, s3_:s4].T),
            q_norm[l], kv_norm[l], bf(w_qt), bf(w_knt), bf(w_vt), cs,
            batch=batch, seq=seq, q_scale=q_scale)
        f = _seq_dft(ab, wa, tc, ts, wb, batch=batch, seq=seq)
        ot = _attention(qt, k, vt, batch=batch, seq=seq)
        x2d = _merge(x2d, mod, mix_norm[l], f, ot, bf(w_in_l[:, s4:]), bf(w_fourier_out[l]),
                     bf(w_mla_out[l]), bf(w_out[l]), seq=seq)

        x2d = _ffn(x2d, mod, ffn2_norm[l], bf(ffn2_w_gate[l]), bf(ffn2_w_up[l]), bf(ffn2_w_down[l]),
                   final_norm, sub=2, final=(l == depth - 1), seq=seq)
    return x2d.reshape(batch, seq, d)
```

```python
import functools

import numpy as np
import jax
import jax.numpy as jnp
from jax import lax
from jax.experimental import pallas as pl
from jax.experimental.pallas import tpu as pltpu

F32 = jnp.float32
BF16 = jnp.bfloat16

FOURIER_GROUPS = 4
FOURIER_GROUP_DIM = 128
FOURIER_WIDTH = FOURIER_GROUPS * FOURIER_GROUP_DIM
MLA_HEADS = 8
QK_NOPE_DIM = 64
QK_ROPE_DIM = 32
QK_DIM = QK_NOPE_DIM + QK_ROPE_DIM
V_HEAD_DIM = 64
Q_LORA_RANK = 384
KV_LORA_RANK = 256
ROPE_THETA = 10000.0
NORM_EPS = 1e-6
N_ADA = 9

HEAD_PAD = 128
ROPE_HALF = QK_ROPE_DIM // 2
FFT_N1 = 128
ROW_TILE = 512
MIX_SLABS = 1
FFN_TILE = 512
ROW_SLABS = 2
KV_TILE = 256
Q_TILE = 512
ATTN_Q_UNROLL = 2
ATTN_AHEAD = 2
ATTN_LAG = 2
ATTN_MAX_EXCESS = 60.0
FFT_PAIRS = 8
VMEM_LIMIT = 56 * 1024 * 1024


def _params(n_axes, flags=None):
    return pltpu.CompilerParams(
        dimension_semantics=("parallel",) * n_axes, vmem_limit_bytes=VMEM_LIMIT, flags=flags)


def _const_spec(shape):
    nd = len(shape)
    return pl.BlockSpec(shape, lambda *_: (0,) * nd, pipeline_mode=pl.Buffered(1))


def _rms(x, g):
    ms = jnp.mean(x * x, axis=-1, keepdims=True)
    return x * lax.rsqrt(ms + NORM_EPS) * g


def _norm_mod(x, g, shift, scale):
    return _rms(x, g) * (1.0 + scale) + shift


def _dot(a, b):
    return jnp.dot(a, b, preferred_element_type=F32)


def _dot_nt(a, b):
    return lax.dot_general(a, b, (((1,), (1,)), ((), ())), preferred_element_type=F32)


def _pair_rows(x):
    return pltpu.bitcast(x.astype(BF16), jnp.uint32)


def _dot_tn(a, b):
    return lax.dot_general(a, b, (((0,), (0,)), ((), ())), preferred_element_type=F32)


def _ada_kernel(c_ref, w_ref, b_ref, o_ref):
    c = c_ref[...]
    ca = c * jax.nn.sigmoid(c)
    o_ref[...] = jnp.dot(ca, w_ref[...], preferred_element_type=F32,
                         precision=lax.Precision.HIGHEST) + b_ref[...]


def _ada(c, ada_w, ada_b):
    b, d = c.shape
    rows = 8
    c_pad = jnp.zeros((rows, d), F32).at[:b].set(c)
    out = pl.pallas_call(
        _ada_kernel,
        grid=(N_ADA,),
        in_specs=[pl.BlockSpec((rows, d), lambda j: (0, 0)),
                  pl.BlockSpec((d, d), lambda j: (0, j)),
                  pl.BlockSpec((1, d), lambda j: (0, j))],
        out_specs=pl.BlockSpec((rows, d), lambda j: (0, j)),
        out_shape=jax.ShapeDtypeStruct((rows, N_ADA * d), F32),
        compiler_params=_params(1),
        name="ada",
    )(c_pad, ada_w, ada_b.reshape(1, -1))
    return out[:b].reshape(b, N_ADA, d)


def _ffn_kernel(x_ref, mod_ref, g_ref, wg_ref, wu_ref, wd_ref, fn_ref, o_ref, *, sub, final):
    shift = mod_ref[3 * sub:3 * sub + 1, :]
    scale = mod_ref[3 * sub + 1:3 * sub + 2, :]
    gate = mod_ref[3 * sub + 2:3 * sub + 3, :]
    slab = x_ref.shape[0] // ROW_SLABS
    for r in range(ROW_SLABS):
        rows = slice(r * slab, (r + 1) * slab)
        x = x_ref[rows, :]
        h = _norm_mod(x, g_ref[...], shift, scale)
        gg = _dot(h, wg_ref[...])
        uu = _dot(h, wu_ref[...])
        a = gg * jax.nn.sigmoid(gg) * uu
        y = x + (0.5 * gate) * _dot(a, wd_ref[...])
        if final:
            y = _rms(y, fn_ref[...])
        o_ref[rows, :] = y


def _ffn(x2d, mod, norm_g, wg, wu, wd, final_g, *, sub, final, seq):
    n, d = x2d.shape
    dff = wg.shape[1]
    per_batch = seq // FFN_TILE
    return pl.pallas_call(
        functools.partial(_ffn_kernel, sub=sub, final=final),
        grid=(n // FFN_TILE,),
        in_specs=[pl.BlockSpec((FFN_TILE, d), lambda i: (i, 0)),
                  pl.BlockSpec((None, N_ADA, d), lambda i: (i // per_batch, 0, 0)),
                  _const_spec((1, d)),
                  _const_spec((d, dff)), _const_spec((d, dff)), _const_spec((dff, d)),
                  _const_spec((1, d))],
        out_specs=pl.BlockSpec((FFN_TILE, d), lambda i: (i, 0)),
        out_shape=jax.ShapeDtypeStruct((n, d), F32),
        compiler_params=_params(1),
        name="ffn%d" % sub,
    )(x2d, mod, norm_g.reshape(1, d), wg, wu, wd, final_g.reshape(1, d))


def _mix_in_kernel(x_ref, mod_ref, g_ref, pos_ref, invf_ref, w_in_ref, w_krt_ref, qn_ref, kvn_ref,
                   w_qt_ref, w_knt_ref, w_vt_ref, cs_ref,
                   ab_ref, qt_ref, k_ref, vt_ref, *, q_scale):
    slab = x_ref.shape[0] // MIX_SLABS
    for r in range(MIX_SLABS):
        _mix_in_slab(r * slab, slab, x_ref, mod_ref, g_ref, pos_ref, invf_ref, w_in_ref, w_krt_ref,
                     qn_ref, kvn_ref, w_qt_ref, w_knt_ref, w_vt_ref, cs_ref,
                     ab_ref, qt_ref, k_ref, vt_ref, q_scale)


def _mix_in_slab(r0, tile, x_ref, mod_ref, g_ref, pos_ref, invf_ref, w_in_ref, w_krt_ref, qn_ref, kvn_ref,
                 w_qt_ref, w_knt_ref, w_vt_ref, cs_ref, ab_ref, qt_ref, k_ref, vt_ref, q_scale):
    rows = slice(r0, r0 + tile)
    x = x_ref[rows, :]
    h = _norm_mod(x, g_ref[...], mod_ref[3:4, :], mod_ref[4:5, :])
    z = _dot(h, w_in_ref[...])

    cs = cs_ref[...]
    pair_rows = slice(r0 // 2, (r0 + tile) // 2)
    for g in range(FOURIER_GROUPS):
        lo = g * FOURIER_GROUP_DIM
        u = z[:, lo:lo + FOURIER_GROUP_DIM].astype(BF16)
        ab = _pair_rows(_dot(u, cs))
        ab_ref[pair_rows, lo:lo + FOURIER_GROUP_DIM] = ab[:, :FOURIER_GROUP_DIM]
        ab_ref[pair_rows, FOURIER_WIDTH + lo:FOURIER_WIDTH + lo + FOURIER_GROUP_DIM] = (
            ab[:, FOURIER_GROUP_DIM:])

    ang = invf_ref[...] * pos_ref[:, rows]
    cos_t = jnp.cos(ang)
    sin_t = jnp.sin(ang)

    def rope_t(v):
        v1, v2 = v[:ROPE_HALF], v[ROPE_HALF:]
        return v1 * cos_t - v2 * sin_t, v2 * cos_t + v1 * sin_t

    pad = jnp.zeros((HEAD_PAD - QK_DIM, tile), F32)

    q_lat = z[:, FOURIER_WIDTH:FOURIER_WIDTH + Q_LORA_RANK]
    qn = _rms(q_lat, qn_ref[...])
    qt = _dot_nt(w_qt_ref[...], qn)
    for hd in range(MLA_HEADS):
        lo = hd * QK_DIM
        r1, r2 = rope_t(qt[lo + QK_NOPE_DIM:lo + QK_DIM])
        qh = jnp.concatenate([qt[lo:lo + QK_NOPE_DIM], r1, r2], axis=0) * q_scale
        qt_ref[hd, :, rows] = jnp.concatenate([qh, pad], axis=0).astype(BF16)

    kv_lat = z[:, FOURIER_WIDTH + Q_LORA_RANK:]
    kvn = _rms(kv_lat, kvn_ref[...])
    knt = _dot_nt(w_knt_ref[...], kvn)
    vt = _dot_nt(w_vt_ref[...], kvn)
    kr1, kr2 = rope_t(_dot_nt(w_krt_ref[...], h))
    for hd in range(MLA_HEADS):
        lo = hd * QK_NOPE_DIM
        kt = jnp.concatenate([knt[lo:lo + QK_NOPE_DIM], kr1, kr2, pad], axis=0)
        k_ref[rows, hd * HEAD_PAD:(hd + 1) * HEAD_PAD] = kt.T.astype(BF16)
        for cch in range(tile // KV_TILE):
            vt_ref[hd, r0 // KV_TILE + cch] = vt[hd * V_HEAD_DIM:(hd + 1) * V_HEAD_DIM,
                                                 cch * KV_TILE:(cch + 1) * KV_TILE].astype(BF16)


def _mix_in(x2d, mod, norm_g, pos_rows, inv_freq, w_in_a, w_krt, q_norm, kv_norm, w_qt, w_knt, w_vt,
            cs, *, batch, seq, q_scale):
    n, d = x2d.shape
    per_batch = seq // ROW_TILE
    assert ROW_TILE % KV_TILE == 0
    chunks_per_tile = ROW_TILE // KV_TILE
    ispec = [pl.BlockSpec((ROW_TILE, d), lambda i: (i, 0)),
             pl.BlockSpec((None, N_ADA, d), lambda i: (i // per_batch, 0, 0)),
             _const_spec((1, d)),
             pl.BlockSpec((None, 1, ROW_TILE), lambda i: (i, 0, 0)),
             _const_spec(inv_freq.shape),
             _const_spec(w_in_a.shape), _const_spec(w_krt.shape),
             _const_spec((1, Q_LORA_RANK)), _const_spec((1, KV_LORA_RANK)),
             _const_spec(w_qt.shape), _const_spec(w_knt.shape), _const_spec(w_vt.shape),
             _const_spec(cs.shape)]
    hp = MLA_HEADS * HEAD_PAD
    ospec = [pl.BlockSpec((ROW_TILE // 2, 2 * FOURIER_WIDTH), lambda i: (i, 0)),
             pl.BlockSpec((None, MLA_HEADS, None, HEAD_PAD, Q_TILE),
                          lambda i: (i // per_batch, 0, i % per_batch, 0, 0)),
             pl.BlockSpec((ROW_TILE, hp), lambda i: (i, 0)),
             pl.BlockSpec((None, MLA_HEADS, chunks_per_tile, V_HEAD_DIM, KV_TILE),
                          lambda i: (i // per_batch, 0, i % per_batch, 0, 0))]
    oshape = [jax.ShapeDtypeStruct((n // 2, 2 * FOURIER_WIDTH), jnp.uint32),
              jax.ShapeDtypeStruct((batch, MLA_HEADS, seq // Q_TILE, HEAD_PAD, Q_TILE), BF16),
              jax.ShapeDtypeStruct((n, hp), BF16),
              jax.ShapeDtypeStruct((batch, MLA_HEADS, seq // KV_TILE, V_HEAD_DIM, KV_TILE), BF16)]
    return pl.pallas_call(
        functools.partial(_mix_in_kernel, q_scale=q_scale),
        grid=(n // ROW_TILE,),
        in_specs=ispec, out_specs=ospec, out_shape=oshape,
        compiler_params=_params(1),
        name="mix_in",
    )(x2d, mod, norm_g.reshape(1, d), pos_rows, inv_freq, w_in_a, w_krt,
      q_norm.reshape(1, -1), kv_norm.reshape(1, -1), w_qt, w_knt, w_vt, cs)


def _paired_dft(words, w):
    x = pltpu.bitcast(words, BF16)
    k = x.shape[0]
    return _dot(w[:, :k], x[:, :FOURIER_WIDTH]) + _dot(w[:, k:], x[:, FOURIER_WIDTH:])


def _fft_a_kernel(x_ref, w_ref, tc_ref, ts_ref, o_ref):
    w = w_ref[...]
    n2 = w.shape[0] // 4
    for j in range(FFT_PAIRS):
        g = _paired_dft(x_ref[:, j, :], w)
        for par in range(2):
            gr = g[2 * par * n2:(2 * par + 1) * n2]
            gi = g[(2 * par + 1) * n2:(2 * par + 2) * n2]
            tc = jnp.concatenate([tc_ref[2 * j + par]] * FOURIER_GROUPS, axis=1)
            ts = jnp.concatenate([ts_ref[2 * j + par]] * FOURIER_GROUPS, axis=1)
            h = jnp.concatenate([gr * tc - gi * ts, gi * tc + gr * ts], axis=1)
            o_ref[2 * j + par] = _pair_rows(h)


def _fft_b_kernel(x_ref, w_ref, o_ref):
    w = w_ref[...]
    n1 = w.shape[0] // 2
    for j in range(FFT_PAIRS):
        f = _paired_dft(x_ref[:, j, :], w)
        o_ref[2 * j] = f[:n1]
        o_ref[2 * j + 1] = f[n1:]


def _seq_dft(ab_words, wa, tc, ts, wb, *, batch, seq):
    n1, n2 = FFT_N1, seq // FFT_N1
    wd = 2 * FOURIER_WIDTH
    x = ab_words.reshape(batch, n2, n1 // 2, wd)
    rows = 2 * FFT_PAIRS
    y = pl.pallas_call(
        _fft_a_kernel,
        grid=(batch, n1 // rows),
        in_specs=[pl.BlockSpec((None, n2, FFT_PAIRS, wd), lambda b, u: (b, 0, u, 0)),
                  _const_spec(wa.shape),
                  pl.BlockSpec((rows, n2, 128), lambda b, u: (u, 0, 0)),
                  pl.BlockSpec((rows, n2, 128), lambda b, u: (u, 0, 0))],
        out_specs=pl.BlockSpec((None, rows, n2 // 2, wd), lambda b, u: (b, u, 0, 0)),
        out_shape=jax.ShapeDtypeStruct((batch, n1, n2 // 2, wd), jnp.uint32),
        compiler_params=_params(2),
        name="fft_a",
    )(x, wa, tc, ts)
    return pl.pallas_call(
        _fft_b_kernel,
        grid=(batch, n2 // rows),
        in_specs=[pl.BlockSpec((None, n1, FFT_PAIRS, wd), lambda b, v: (b, 0, v, 0)),
                  _const_spec(wb.shape)],
        out_specs=pl.BlockSpec((None, rows, n1, FOURIER_WIDTH), lambda b, v: (b, v, 0, 0)),
        out_shape=jax.ShapeDtypeStruct((batch, n2, n1, FOURIER_WIDTH), F32),
        compiler_params=_params(2),
        name="fft_b",
    )(y, wb)


def _dft_constants(seq):
    n1, n2 = FFT_N1, seq // FFT_N1
    gd = FOURIER_GROUP_DIM
    i = np.arange(gd)
    ang_c = 2.0 * np.pi * np.outer(i, i) / gd
    cs = np.concatenate([np.cos(ang_c), np.sin(ang_c)], axis=1)
    i2 = np.arange(n2)
    ang_a = 2.0 * np.pi * np.outer(i2, i2) / n2
    ca, sa = np.cos(ang_a), np.sin(ang_a)
    i1 = np.arange(n1)
    ang_t = 2.0 * np.pi * np.outer(i1, i2) / seq
    ang_b = 2.0 * np.pi * np.outer(i1, i1) / n1
    norm = 1.0 / np.sqrt(float(seq) * gd)
    cb, sb = np.cos(ang_b) * norm, np.sin(ang_b) * norm

    def spread(re_part, im_part):
        m, r = re_part.shape
        out = np.zeros((2, m, 2, 2 * r))
        for p in range(2):
            out[p, :, 0, p::2] = re_part
            out[p, :, 1, p::2] = im_part
        return out.reshape(2 * m, 4 * r)

    wa = spread(np.concatenate([ca, sa], axis=0), np.concatenate([-sa, ca], axis=0))
    wb = spread(cb, -sb)
    f32 = lambda a: jnp.asarray(np.asarray(a, np.float32))
    tc = jnp.broadcast_to(f32(np.cos(ang_t))[:, :, None], (n1, n2, 128))
    ts = jnp.broadcast_to(f32(np.sin(ang_t))[:, :, None], (n1, n2, 128))
    return f32(cs).astype(BF16), f32(wa).astype(BF16), tc, ts, f32(wb).astype(BF16)


def _attn_kernel(qt_ref, k_ref, vt_ref, o_ref):
    n_q, _, tq = qt_ref.shape
    n_chunks = vt_ref.shape[0]
    ones = jnp.ones((16, KV_TILE), BF16)

    def pv(j, p):
        return _dot(jnp.concatenate([vt_ref[j], ones], axis=0), p)

    def finish(acc):
        return (acc[:V_HEAD_DIM] / acc[V_HEAD_DIM:V_HEAD_DIM + 1]).astype(BF16)

    def streaming_tile(t):
        qt = qt_ref[t]

        def qk(j):
            return _dot(k_ref[j * KV_TILE:(j + 1) * KV_TILE, :], qt)

        s0 = qk(0)
        m = jnp.max(s0, axis=0, keepdims=True)
        inflight = [qk(j) for j in range(1, 1 + ATTN_AHEAD)]
        acc = pv(0, jnp.exp2(s0 - m).astype(BF16))
        excess = jnp.zeros((1, tq), F32)
        pending = []
        for j in range(1, n_chunks):
            s = inflight.pop(0)
            if j + ATTN_AHEAD < n_chunks:
                inflight.append(qk(j + ATTN_AHEAD))
            cm = jnp.max(s, axis=0, keepdims=True)
            acc = acc + pv(j, jnp.exp2(s - m).astype(BF16))
            excess = jnp.maximum(excess, cm - m)
            pending.append(cm)
            if len(pending) >= ATTN_LAG:
                m_new = jnp.maximum(m, pending.pop(0))
                acc = acc * jnp.exp2(m - m_new)
                m = m_new
        o_ref[t] = finish(acc)
        return excess

    def two_pass_tile(t):
        qt = qt_ref[t]

        def body(j, carry):
            m_c, acc_c = carry
            kc = k_ref[pl.ds(pl.multiple_of(j * KV_TILE, KV_TILE), KV_TILE), :]
            s_c = _dot(kc, qt)
            m_n = jnp.maximum(m_c, jnp.max(s_c, axis=0, keepdims=True))
            p_c = jnp.exp2(s_c - m_n).astype(BF16)
            return m_n, jnp.exp2(m_c - m_n) * acc_c + pv(j, p_c)

        m_0 = jnp.full((1, tq), -jnp.inf, F32)
        acc_0 = jnp.zeros((V_HEAD_DIM + 16, tq), F32)
        _, acc_s = lax.fori_loop(0, n_chunks, body, (m_0, acc_0))
        o_ref[t] = finish(acc_s)

    def trip(i, carry):
        tiles = [i * ATTN_Q_UNROLL + u for u in range(ATTN_Q_UNROLL)]
        excess = [streaming_tile(t) for t in tiles]
        for t, e in zip(tiles, excess):
            pl.when(jnp.max(e) > ATTN_MAX_EXCESS)(functools.partial(two_pass_tile, t))
        return carry

    lax.fori_loop(0, n_q // ATTN_Q_UNROLL, trip, 0)


def _attention(qt, k, vt, *, batch, seq):
    n_chunks = seq // KV_TILE
    n_q = seq // Q_TILE
    assert n_q % ATTN_Q_UNROLL == 0
    return pl.pallas_call(
        _attn_kernel,
        grid=(batch, MLA_HEADS),
        in_specs=[pl.BlockSpec((None, None, n_q, HEAD_PAD, Q_TILE), lambda b, h: (b, h, 0, 0, 0)),
                  pl.BlockSpec((None, seq, HEAD_PAD), lambda b, h: (b, 0, h)),
                  pl.BlockSpec((None, None, n_chunks, V_HEAD_DIM, KV_TILE),
                               lambda b, h: (b, h, 0, 0, 0))],
        out_specs=pl.BlockSpec((None, None, n_q, V_HEAD_DIM, Q_TILE), lambda b, h: (b, h, 0, 0, 0)),
        out_shape=jax.ShapeDtypeStruct((batch, MLA_HEADS, n_q, V_HEAD_DIM, Q_TILE), BF16),
        compiler_params=_params(2),
        name="attn",
    )(qt, k.reshape(batch, seq, MLA_HEADS * HEAD_PAD), vt)


def _merge_kernel(x_ref, mod_ref, g_ref, f_ref, ot_ref, wg_ref, wfo_ref, wmo_ref, wout_ref, o_ref):
    x = x_ref[...]
    d = x.shape[1]
    h = _norm_mod(x, g_ref[...], mod_ref[3:4, :], mod_ref[4:5, :])
    gl = _dot(h, wg_ref[...])
    f = jnp.concatenate([f_ref[:, j, :] for j in range(f_ref.shape[1])], axis=0)
    y_a = _dot(f, wfo_ref[...])
    ot = ot_ref[...]
    y_b = _dot_tn(ot.reshape(ot.shape[0] * ot.shape[1], ot.shape[2]), wmo_ref[...])
    y = jax.nn.sigmoid(gl[:, :d]) * y_a + jax.nn.sigmoid(gl[:, d:]) * y_b
    o_ref[...] = x + mod_ref[5:6, :] * _dot(y, wout_ref[...])


def _merge(x2d, mod, norm_g, f, ot, w_g, w_fo, w_mo, w_out, *, seq):
    n, d = x2d.shape
    per_batch = seq // ROW_TILE
    n2 = f.shape[1]
    assert ROW_TILE == FFT_PAIRS * n2 and ROW_TILE == Q_TILE
    return pl.pallas_call(
        _merge_kernel,
        grid=(n // ROW_TILE,),
        in_specs=[pl.BlockSpec((ROW_TILE, d), lambda i: (i, 0)),
                  pl.BlockSpec((None, N_ADA, d), lambda i: (i // per_batch, 0, 0)),
                  _const_spec((1, d)),
                  pl.BlockSpec((None, n2, ROW_TILE // n2, FOURIER_WIDTH),
                               lambda i: (i // per_batch, 0, i % per_batch, 0)),
                  pl.BlockSpec((None, MLA_HEADS, None, V_HEAD_DIM, Q_TILE),
                               lambda i: (i // per_batch, 0, i % per_batch, 0, 0)),
                  _const_spec(w_g.shape), _const_spec(w_fo.shape), _const_spec(w_mo.shape),
                  _const_spec(w_out.shape)],
        out_specs=pl.BlockSpec((ROW_TILE, d), lambda i: (i, 0)),
        out_shape=jax.ShapeDtypeStruct((n, d), F32),
        compiler_params=_params(1),
        name="merge",
    )(x2d, mod, norm_g.reshape(1, d), f, ot, w_g, w_fo, w_mo, w_out)


def kernel(x, c, positions, ada_w, ada_b, ffn1_norm, ffn1_w_gate, ffn1_w_up, ffn1_w_down, mix_norm,
           w_in, q_norm, w_q_up, kv_norm, w_kv_up, w_fourier_out, w_mla_out, w_out, ffn2_norm,
           ffn2_w_gate, ffn2_w_up, ffn2_w_down, final_norm):
    batch, seq, d = x.shape
    depth = ada_w.shape[0]
    x2d = x.reshape(batch * seq, d)
    pos_rows = positions.astype(F32).reshape(batch * seq // ROW_TILE, 1, ROW_TILE)
    inv_freq = (1.0 / (ROPE_THETA ** (jnp.arange(ROPE_HALF, dtype=F32) * 2.0 / QK_ROPE_DIM)))
    inv_freq = inv_freq.reshape(ROPE_HALF, 1)
    q_scale = float(QK_DIM ** -0.5 * np.log2(np.e))
    cs, wa, tc, ts, wb = _dft_constants(seq)
    s1 = FOURIER_WIDTH
    s2 = s1 + Q_LORA_RANK
    s3_ = s2 + KV_LORA_RANK
    s4 = s3_ + QK_ROPE_DIM

    for l in range(depth):
        mod = _ada(c, ada_w[l], ada_b[l])
        x2d = _ffn(x2d, mod, ffn1_norm[l], ffn1_w_gate[l], ffn1_w_up[l], ffn1_w_down[l],
                   final_norm, sub=0, final=False, seq=seq)

        w_in_l = w_in[l]
        w_qt = w_q_up[l].T.reshape(MLA_HEADS, QK_DIM, Q_LORA_RANK).reshape(MLA_HEADS * QK_DIM, -1)
        w_kv = w_kv_up[l].reshape(KV_LORA_RANK, MLA_HEADS, QK_NOPE_DIM + V_HEAD_DIM)
        w_knt = w_kv[:, :, :QK_NOPE_DIM].reshape(KV_LORA_RANK, -1).T
        w_vt = w_kv[:, :, QK_NOPE_DIM:].reshape(KV_LORA_RANK, -1).T
        ab, qt, k, vt = _mix_in(
            x2d, mod, mix_norm[l], pos_rows, inv_freq, w_in_l[:, :s3_], w_in_l[:, s3_:s4].T,
            q_norm[l], kv_norm[l], w_qt, w_knt, w_vt, cs,
            batch=batch, seq=seq, q_scale=q_scale)
        f = _seq_dft(ab, wa, tc, ts, wb, batch=batch, seq=seq)
        ot = _attention(qt, k, vt, batch=batch, seq=seq)
        x2d = _merge(x2d, mod, mix_norm[l], f, ot, w_in_l[:, s4:], w_fourier_out[l],
                     w_mla_out[l].astype(BF16), w_out[l], seq=seq)

        x2d = _ffn(x2d, mod, ffn2_norm[l], ffn2_w_gate[l], ffn2_w_up[l], ffn2_w_down[l],
                   final_norm, sub=2, final=(l == depth - 1), seq=seq)
    return x2d.reshape(batch, seq, d)
```

```python
import functools

import numpy as np
import jax
import jax.numpy as jnp
from jax import lax
from jax.experimental import pallas as pl
from jax.experimental.pallas import tpu as pltpu

F32 = jnp.float32
BF16 = jnp.bfloat16

FOURIER_GROUPS = 4
FOURIER_GROUP_DIM = 128
FOURIER_WIDTH = FOURIER_GROUPS * FOURIER_GROUP_DIM
MLA_HEADS = 8
QK_NOPE_DIM = 64
QK_ROPE_DIM = 32
QK_DIM = QK_NOPE_DIM + QK_ROPE_DIM
V_HEAD_DIM = 64
Q_LORA_RANK = 384
KV_LORA_RANK = 256
ROPE_THETA = 10000.0
NORM_EPS = 1e-6
N_ADA = 9

HEAD_PAD = 128
ROPE_HALF = QK_ROPE_DIM // 2
FFT_N1 = 128
ROW_TILE = 512
MIX_SLABS = 1
FFN_TILE = 512
ROW_SLABS = 2
KV_TILE = 256
Q_TILE = 512
ATTN_Q_UNROLL = 2
ATTN_AHEAD = 2
ATTN_LAG = 2
ATTN_MAX_EXCESS = 60.0
FFT_PAIRS = 8
VMEM_LIMIT = 56 * 1024 * 1024


def _params(n_axes, flags=None):
    return pltpu.CompilerParams(
        dimension_semantics=("parallel",) * n_axes, vmem_limit_bytes=VMEM_LIMIT, flags=flags)


def _const_spec(shape):
    nd = len(shape)
    return pl.BlockSpec(shape, lambda *_: (0,) * nd, pipeline_mode=pl.Buffered(1))


def _rms(x, g):
    ms = jnp.mean(x * x, axis=-1, keepdims=True)
    return x * lax.rsqrt(ms + NORM_EPS) * g


def _norm_mod(x, g, shift, scale):
    return _rms(x, g) * (1.0 + scale) + shift


def _dot(a, b):
    return jnp.dot(a, b, preferred_element_type=F32)


def _dot_nt(a, b):
    return lax.dot_general(a, b, (((1,), (1,)), ((), ())), preferred_element_type=F32)


def _pair_rows(x):
    return pltpu.bitcast(x.astype(BF16), jnp.uint32)


def _dot_tn(a, b):
    return lax.dot_general(a, b, (((0,), (0,)), ((), ())), preferred_element_type=F32)


def _ada_kernel(c_ref, w_ref, b_ref, o_ref):
    c = c_ref[...]
    ca = c * jax.nn.sigmoid(c)
    w = w_ref[...]
    ca_hi = ca.astype(BF16)
    ca_lo = (ca - ca_hi.astype(F32)).astype(BF16)
    w_hi = w.astype(BF16)
    w_lo = (w - w_hi.astype(F32)).astype(BF16)
    o_ref[...] = (_dot(ca_hi, w_hi) + (_dot(ca_hi, w_lo) + _dot(ca_lo, w_hi))) + b_ref[...]


def _ada(c, ada_w, ada_b):
    b, d = c.shape
    rows = 8
    c_pad = jnp.zeros((rows, d), F32).at[:b].set(c)
    out = pl.pallas_call(
        _ada_kernel,
        grid=(N_ADA,),
        in_specs=[pl.BlockSpec((rows, d), lambda j: (0, 0)),
                  pl.BlockSpec((d, d), lambda j: (0, j)),
                  pl.BlockSpec((1, d), lambda j: (0, j))],
        out_specs=pl.BlockSpec((rows, d), lambda j: (0, j)),
        out_shape=jax.ShapeDtypeStruct((rows, N_ADA * d), F32),
        compiler_params=_params(1),
        name="ada",
    )(c_pad, ada_w, ada_b.reshape(1, -1))
    return out[:b].reshape(b, N_ADA, d)


def _ffn_kernel(x_ref, mod_ref, g_ref, wg_ref, wu_ref, wd_ref, fn_ref, o_ref, *, sub, final):
    shift = mod_ref[3 * sub:3 * sub + 1, :]
    scale = mod_ref[3 * sub + 1:3 * sub + 2, :]
    gate = mod_ref[3 * sub + 2:3 * sub + 3, :]
    slab = x_ref.shape[0] // ROW_SLABS
    for r in range(ROW_SLABS):
        rows = slice(r * slab, (r + 1) * slab)
        x = x_ref[rows, :]
        h = _norm_mod(x, g_ref[...], shift, scale)
        gg = _dot(h, wg_ref[...])
        uu = _dot(h, wu_ref[...])
        a = gg * jax.nn.sigmoid(gg) * uu
        y = x + (0.5 * gate) * _dot(a, wd_ref[...])
        if final:
            y = _rms(y, fn_ref[...])
        o_ref[rows, :] = y


def _ffn(x2d, mod, norm_g, wg, wu, wd, final_g, *, sub, final, seq):
    n, d = x2d.shape
    dff = wg.shape[1]
    per_batch = seq // FFN_TILE
    return pl.pallas_call(
        functools.partial(_ffn_kernel, sub=sub, final=final),
        grid=(n // FFN_TILE,),
        in_specs=[pl.BlockSpec((FFN_TILE, d), lambda i: (i, 0)),
                  pl.BlockSpec((None, N_ADA, d), lambda i: (i // per_batch, 0, 0)),
                  _const_spec((1, d)),
                  _const_spec((d, dff)), _const_spec((d, dff)), _const_spec((dff, d)),
                  _const_spec((1, d))],
        out_specs=pl.BlockSpec((FFN_TILE, d), lambda i: (i, 0)),
        out_shape=jax.ShapeDtypeStruct((n, d), F32),
        compiler_params=_params(1),
        name="ffn%d" % sub,
    )(x2d, mod, norm_g.reshape(1, d), wg, wu, wd, final_g.reshape(1, d))


def _mix_in_kernel(x_ref, mod_ref, g_ref, pos_ref, invf_ref, w_in_ref, qn_ref, kvn_ref,
                   w_qt_ref, w_knt_ref, w_vt_ref, cs_ref,
                   ab_ref, qt_ref, k_ref, vt_ref, *, q_scale):
    slab = x_ref.shape[0] // MIX_SLABS
    for r in range(MIX_SLABS):
        _mix_in_slab(r * slab, slab, x_ref, mod_ref, g_ref, pos_ref, invf_ref, w_in_ref,
                     qn_ref, kvn_ref, w_qt_ref, w_knt_ref, w_vt_ref, cs_ref,
                     ab_ref, qt_ref, k_ref, vt_ref, q_scale)


def _mix_in_slab(r0, tile, x_ref, mod_ref, g_ref, pos_ref, invf_ref, w_in_ref, qn_ref, kvn_ref,
                 w_qt_ref, w_knt_ref, w_vt_ref, cs_ref, ab_ref, qt_ref, k_ref, vt_ref, q_scale):
    rows = slice(r0, r0 + tile)
    x = x_ref[rows, :]
    h = _norm_mod(x, g_ref[...], mod_ref[3:4, :], mod_ref[4:5, :])
    n_lat = FOURIER_WIDTH + Q_LORA_RANK + KV_LORA_RANK
    z = _dot_nt(h, w_in_ref[:n_lat, :])

    cs = cs_ref[...]
    pair_rows = slice(r0 // 2, (r0 + tile) // 2)
    for g in range(FOURIER_GROUPS):
        lo = g * FOURIER_GROUP_DIM
        u = z[:, lo:lo + FOURIER_GROUP_DIM].astype(BF16)
        ab = _pair_rows(_dot(u, cs))
        ab_ref[pair_rows, lo:lo + FOURIER_GROUP_DIM] = ab[:, :FOURIER_GROUP_DIM]
        ab_ref[pair_rows, FOURIER_WIDTH + lo:FOURIER_WIDTH + lo + FOURIER_GROUP_DIM] = (
            ab[:, FOURIER_GROUP_DIM:])

    ang = invf_ref[...] * pos_ref[:, rows]
    cos_t = jnp.cos(ang)
    sin_t = jnp.sin(ang)

    def rope_t(v):
        v1, v2 = v[:ROPE_HALF], v[ROPE_HALF:]
        return v1 * cos_t - v2 * sin_t, v2 * cos_t + v1 * sin_t

    pad = jnp.zeros((HEAD_PAD - QK_DIM, tile), F32)

    q_lat = z[:, FOURIER_WIDTH:FOURIER_WIDTH + Q_LORA_RANK]
    qn = _rms(q_lat, qn_ref[...])
    qt = _dot_nt(w_qt_ref[...], qn)
    for hd in range(MLA_HEADS):
        lo = hd * QK_DIM
        r1, r2 = rope_t(qt[lo + QK_NOPE_DIM:lo + QK_DIM])
        qh = jnp.concatenate([qt[lo:lo + QK_NOPE_DIM], r1, r2], axis=0) * q_scale
        qt_ref[hd, :, rows] = jnp.concatenate([qh, pad], axis=0).astype(BF16)

    kv_lat = z[:, FOURIER_WIDTH + Q_LORA_RANK:]
    kvn = _rms(kv_lat, kvn_ref[...])
    knt = _dot_nt(w_knt_ref[...], kvn)
    vt = _dot_nt(w_vt_ref[...], kvn)
    kr1, kr2 = rope_t(_dot_nt(w_in_ref[n_lat:, :], h))
    for hd in range(MLA_HEADS):
        lo = hd * QK_NOPE_DIM
        kt = jnp.concatenate([knt[lo:lo + QK_NOPE_DIM], kr1, kr2, pad], axis=0)
        k_ref[rows, hd * HEAD_PAD:(hd + 1) * HEAD_PAD] = kt.T.astype(BF16)
        for cch in range(tile // KV_TILE):
            vt_ref[hd, r0 // KV_TILE + cch] = vt[hd * V_HEAD_DIM:(hd + 1) * V_HEAD_DIM,
                                                 cch * KV_TILE:(cch + 1) * KV_TILE].astype(BF16)


def _mix_in(x2d, mod, norm_g, pos_rows, inv_freq, w_in_t, q_norm, kv_norm, w_qt, w_knt, w_vt,
            cs, *, batch, seq, q_scale):
    n, d = x2d.shape
    per_batch = seq // ROW_TILE
    assert ROW_TILE % KV_TILE == 0
    chunks_per_tile = ROW_TILE // KV_TILE
    n_in = FOURIER_WIDTH + Q_LORA_RANK + KV_LORA_RANK + QK_ROPE_DIM
    ispec = [pl.BlockSpec((ROW_TILE, d), lambda i: (i, 0)),
             pl.BlockSpec((None, N_ADA, d), lambda i: (i // per_batch, 0, 0)),
             _const_spec((1, d)),
             pl.BlockSpec((None, 1, ROW_TILE), lambda i: (i, 0, 0)),
             _const_spec(inv_freq.shape),
             pl.BlockSpec((n_in, d), lambda i: (0, 0), pipeline_mode=pl.Buffered(1)),
             _const_spec((1, Q_LORA_RANK)), _const_spec((1, KV_LORA_RANK)),
             _const_spec(w_qt.shape), _const_spec(w_knt.shape), _const_spec(w_vt.shape),
             _const_spec(cs.shape)]
    hp = MLA_HEADS * HEAD_PAD
    ospec = [pl.BlockSpec((ROW_TILE // 2, 2 * FOURIER_WIDTH), lambda i: (i, 0)),
             pl.BlockSpec((None, MLA_HEADS, None, HEAD_PAD, Q_TILE),
                          lambda i: (i // per_batch, 0, i % per_batch, 0, 0)),
             pl.BlockSpec((ROW_TILE, hp), lambda i: (i, 0)),
             pl.BlockSpec((None, MLA_HEADS, chunks_per_tile, V_HEAD_DIM, KV_TILE),
                          lambda i: (i // per_batch, 0, i % per_batch, 0, 0))]
    oshape = [jax.ShapeDtypeStruct((n // 2, 2 * FOURIER_WIDTH), jnp.uint32),
              jax.ShapeDtypeStruct((batch, MLA_HEADS, seq // Q_TILE, HEAD_PAD, Q_TILE), BF16),
              jax.ShapeDtypeStruct((n, hp), BF16),
              jax.ShapeDtypeStruct((batch, MLA_HEADS, seq // KV_TILE, V_HEAD_DIM, KV_TILE), BF16)]
    return pl.pallas_call(
        functools.partial(_mix_in_kernel, q_scale=q_scale),
        grid=(n // ROW_TILE,),
        in_specs=ispec, out_specs=ospec, out_shape=oshape,
        compiler_params=_params(1),
        name="mix_in",
    )(x2d, mod, norm_g.reshape(1, d), pos_rows, inv_freq, w_in_t,
      q_norm.reshape(1, -1), kv_norm.reshape(1, -1), w_qt, w_knt, w_vt, cs)


def _paired_dft(words, w):
    x = pltpu.bitcast(words, BF16)
    k = x.shape[0]
    return _dot(w[:, :k], x[:, :FOURIER_WIDTH]) + _dot(w[:, k:], x[:, FOURIER_WIDTH:])


def _fft_a_kernel(x_ref, w_ref, tc_ref, ts_ref, o_ref):
    w = w_ref[...]
    n2 = w.shape[0] // 4
    for j in range(FFT_PAIRS):
        g = _paired_dft(x_ref[:, j, :], w)
        for par in range(2):
            gr = g[2 * par * n2:(2 * par + 1) * n2]
            gi = g[(2 * par + 1) * n2:(2 * par + 2) * n2]
            tc = jnp.concatenate([tc_ref[2 * j + par]] * FOURIER_GROUPS, axis=1)
            ts = jnp.concatenate([ts_ref[2 * j + par]] * FOURIER_GROUPS, axis=1)
            h = jnp.concatenate([gr * tc - gi * ts, gi * tc + gr * ts], axis=1)
            o_ref[2 * j + par] = _pair_rows(h)


def _fft_b_kernel(x_ref, w_ref, o_ref):
    w = w_ref[...]
    n1 = w.shape[0] // 2
    for j in range(FFT_PAIRS):
        f = _paired_dft(x_ref[:, j, :], w)
        o_ref[2 * j] = f[:n1]
        o_ref[2 * j + 1] = f[n1:]


def _seq_dft(ab_words, wa, tc, ts, wb, *, batch, seq):
    n1, n2 = FFT_N1, seq // FFT_N1
    wd = 2 * FOURIER_WIDTH
    x = ab_words.reshape(batch, n2, n1 // 2, wd)
    rows = 2 * FFT_PAIRS
    y = pl.pallas_call(
        _fft_a_kernel,
        grid=(batch, n1 // rows),
        in_specs=[pl.BlockSpec((None, n2, FFT_PAIRS, wd), lambda b, u: (b, 0, u, 0)),
                  _const_spec(wa.shape),
                  pl.BlockSpec((rows, n2, 128), lambda b, u: (u, 0, 0)),
                  pl.BlockSpec((rows, n2, 128), lambda b, u: (u, 0, 0))],
        out_specs=pl.BlockSpec((None, rows, n2 // 2, wd), lambda b, u: (b, u, 0, 0)),
        out_shape=jax.ShapeDtypeStruct((batch, n1, n2 // 2, wd), jnp.uint32),
        compiler_params=_params(2),
        name="fft_a",
    )(x, wa, tc, ts)
    return pl.pallas_call(
        _fft_b_kernel,
        grid=(batch, n2 // rows),
        in_specs=[pl.BlockSpec((None, n1, FFT_PAIRS, wd), lambda b, v: (b, 0, v, 0)),
                  _const_spec(wb.shape)],
        out_specs=pl.BlockSpec((None, rows, n1, FOURIER_WIDTH), lambda b, v: (b, v, 0, 0)),
        out_shape=jax.ShapeDtypeStruct((batch, n2, n1, FOURIER_WIDTH), F32),
        compiler_params=_params(2),
        name="fft_b",
    )(y, wb)


def _dft_constants(seq):
    n1, n2 = FFT_N1, seq // FFT_N1
    gd = FOURIER_GROUP_DIM
    i = np.arange(gd)
    ang_c = 2.0 * np.pi * np.outer(i, i) / gd
    cs = np.concatenate([np.cos(ang_c), np.sin(ang_c)], axis=1)
    i2 = np.arange(n2)
    ang_a = 2.0 * np.pi * np.outer(i2, i2) / n2
    ca, sa = np.cos(ang_a), np.sin(ang_a)
    i1 = np.arange(n1)
    ang_t = 2.0 * np.pi * np.outer(i1, i2) / seq
    ang_b = 2.0 * np.pi * np.outer(i1, i1) / n1
    norm = 1.0 / np.sqrt(float(seq) * gd)
    cb, sb = np.cos(ang_b) * norm, np.sin(ang_b) * norm

    def spread(re_part, im_part):
        m, r = re_part.shape
        out = np.zeros((2, m, 2, 2 * r))
        for p in range(2):
            out[p, :, 0, p::2] = re_part
            out[p, :, 1, p::2] = im_part
        return out.reshape(2 * m, 4 * r)

    wa = spread(np.concatenate([ca, sa], axis=0), np.concatenate([-sa, ca], axis=0))
    wb = spread(cb, -sb)
    f32 = lambda a: jnp.asarray(np.asarray(a, np.float32))
    tc = jnp.broadcast_to(f32(np.cos(ang_t))[:, :, None], (n1, n2, 128))
    ts = jnp.broadcast_to(f32(np.sin(ang_t))[:, :, None], (n1, n2, 128))
    return f32(cs).astype(BF16), f32(wa).astype(BF16), tc, ts, f32(wb).astype(BF16)


def _attn_kernel(qt_ref, k_ref, vt_ref, o_ref):
    n_q, _, tq = qt_ref.shape
    n_chunks = vt_ref.shape[0]
    ones = jnp.ones((16, KV_TILE), BF16)

    def pv(j, p):
        return _dot(jnp.concatenate([vt_ref[j], ones], axis=0), p)

    def finish(acc):
        return (acc[:V_HEAD_DIM] / acc[V_HEAD_DIM:V_HEAD_DIM + 1]).astype(BF16)

    def streaming_tile(t):
        qt = qt_ref[t]

        def qk(j):
            return _dot(k_ref[j * KV_TILE:(j + 1) * KV_TILE, :], qt)

        s0 = qk(0)
        m = jnp.max(s0, axis=0, keepdims=True)
        inflight = [qk(j) for j in range(1, 1 + ATTN_AHEAD)]
        acc = pv(0, jnp.exp2(s0 - m).astype(BF16))
        excess = jnp.zeros((1, tq), F32)
        pending = []
        for j in range(1, n_chunks):
            s = inflight.pop(0)
            if j + ATTN_AHEAD < n_chunks:
                inflight.append(qk(j + ATTN_AHEAD))
            cm = jnp.max(s, axis=0, keepdims=True)
            acc = acc + pv(j, jnp.exp2(s - m).astype(BF16))
            excess = jnp.maximum(excess, cm - m)
            pending.append(cm)
            if len(pending) >= ATTN_LAG:
                m_new = jnp.maximum(m, pending.pop(0))
                acc = acc * jnp.exp2(m - m_new)
                m = m_new
        o_ref[t] = finish(acc)
        return excess

    def two_pass_tile(t):
        qt = qt_ref[t]

        def body(j, carry):
            m_c, acc_c = carry
            kc = k_ref[pl.ds(pl.multiple_of(j * KV_TILE, KV_TILE), KV_TILE), :]
            s_c = _dot(kc, qt)
            m_n = jnp.maximum(m_c, jnp.max(s_c, axis=0, keepdims=True))
            p_c = jnp.exp2(s_c - m_n).astype(BF16)
            return m_n, jnp.exp2(m_c - m_n) * acc_c + pv(j, p_c)

        m_0 = jnp.full((1, tq), -jnp.inf, F32)
        acc_0 = jnp.zeros((V_HEAD_DIM + 16, tq), F32)
        _, acc_s = lax.fori_loop(0, n_chunks, body, (m_0, acc_0))
        o_ref[t] = finish(acc_s)

    def trip(i, carry):
        tiles = [i * ATTN_Q_UNROLL + u for u in range(ATTN_Q_UNROLL)]
        excess = [streaming_tile(t) for t in tiles]
        for t, e in zip(tiles, excess):
            pl.when(jnp.max(e) > ATTN_MAX_EXCESS)(functools.partial(two_pass_tile, t))
        return carry

    lax.fori_loop(0, n_q // ATTN_Q_UNROLL, trip, 0)


def _attention(qt, k, vt, *, batch, seq):
    n_chunks = seq // KV_TILE
    n_q = seq // Q_TILE
    assert n_q % ATTN_Q_UNROLL == 0
    return pl.pallas_call(
        _attn_kernel,
        grid=(batch, MLA_HEADS),
        in_specs=[pl.BlockSpec((None, None, n_q, HEAD_PAD, Q_TILE), lambda b, h: (b, h, 0, 0, 0)),
                  pl.BlockSpec((None, seq, HEAD_PAD), lambda b, h: (b, 0, h)),
                  pl.BlockSpec((None, None, n_chunks, V_HEAD_DIM, KV_TILE),
                               lambda b, h: (b, h, 0, 0, 0))],
        out_specs=pl.BlockSpec((None, None, n_q, V_HEAD_DIM, Q_TILE), lambda b, h: (b, h, 0, 0, 0)),
        out_shape=jax.ShapeDtypeStruct((batch, MLA_HEADS, n_q, V_HEAD_DIM, Q_TILE), BF16),
        compiler_params=_params(2),
        name="attn",
    )(qt, k.reshape(batch, seq, MLA_HEADS * HEAD_PAD), vt)


def _merge_kernel(x_ref, mod_ref, g_ref, f_ref, ot_ref, wg_ref, wfo_ref, wmo_ref, wout_ref, o_ref):
    x = x_ref[...]
    d = x.shape[1]
    h = _norm_mod(x, g_ref[...], mod_ref[3:4, :], mod_ref[4:5, :])
    gl = _dot_nt(h, wg_ref[wg_ref.shape[0] - 2 * d:, :])
    f = jnp.concatenate([f_ref[:, j, :] for j in range(f_ref.shape[1])], axis=0)
    y_a = _dot(f, wfo_ref[...])
    ot = ot_ref[...]
    y_b = _dot_tn(ot.reshape(ot.shape[0] * ot.shape[1], ot.shape[2]), wmo_ref[...])
    y = jax.nn.sigmoid(gl[:, :d]) * y_a + jax.nn.sigmoid(gl[:, d:]) * y_b
    o_ref[...] = x + mod_ref[5:6, :] * _dot(y, wout_ref[...])


def _merge(x2d, mod, norm_g, f, ot, w_g, w_fo, w_mo, w_out, *, seq):
    n, d = x2d.shape
    per_batch = seq // ROW_TILE
    n2 = f.shape[1]
    assert ROW_TILE == FFT_PAIRS * n2 and ROW_TILE == Q_TILE
    return pl.pallas_call(
        _merge_kernel,
        grid=(n // ROW_TILE,),
        in_specs=[pl.BlockSpec((ROW_TILE, d), lambda i: (i, 0)),
                  pl.BlockSpec((None, N_ADA, d), lambda i: (i // per_batch, 0, 0)),
                  _const_spec((1, d)),
                  pl.BlockSpec((None, n2, ROW_TILE // n2, FOURIER_WIDTH),
                               lambda i: (i // per_batch, 0, i % per_batch, 0)),
                  pl.BlockSpec((None, MLA_HEADS, None, V_HEAD_DIM, Q_TILE),
                               lambda i: (i // per_batch, 0, i % per_batch, 0, 0)),
                  _const_spec(w_g.shape), _const_spec(w_fo.shape), _const_spec(w_mo.shape),
                  _const_spec(w_out.shape)],
        out_specs=pl.BlockSpec((ROW_TILE, d), lambda i: (i, 0)),
        out_shape=jax.ShapeDtypeStruct((n, d), F32),
        compiler_params=_params(1),
        name="merge",
    )(x2d, mod, norm_g.reshape(1, d), f, ot, w_g, w_fo, w_mo, w_out)


def kernel(x, c, positions, ada_w, ada_b, ffn1_norm, ffn1_w_gate, ffn1_w_up, ffn1_w_down, mix_norm,
           w_in, q_norm, w_q_up, kv_norm, w_kv_up, w_fourier_out, w_mla_out, w_out, ffn2_norm,
           ffn2_w_gate, ffn2_w_up, ffn2_w_down, final_norm):
    batch, seq, d = x.shape
    depth = ada_w.shape[0]
    x2d = x.reshape(batch * seq, d)
    pos_rows = positions.astype(F32).reshape(batch * seq // ROW_TILE, 1, ROW_TILE)
    inv_freq = (1.0 / (ROPE_THETA ** (jnp.arange(ROPE_HALF, dtype=F32) * 2.0 / QK_ROPE_DIM)))
    inv_freq = inv_freq.reshape(ROPE_HALF, 1)
    q_scale = float(QK_DIM ** -0.5 * np.log2(np.e))
    cs, wa, tc, ts, wb = _dft_constants(seq)

    for l in range(depth):
        mod = _ada(c, ada_w[l], ada_b[l])
        x2d = _ffn(x2d, mod, ffn1_norm[l], ffn1_w_gate[l], ffn1_w_up[l], ffn1_w_down[l],
                   final_norm, sub=0, final=False, seq=seq)

        w_in_t = jnp.swapaxes(w_in, 1, 2)[l]
        w_qt = w_q_up[l].T.reshape(MLA_HEADS, QK_DIM, Q_LORA_RANK).reshape(MLA_HEADS * QK_DIM, -1)
        w_kv = w_kv_up[l].reshape(KV_LORA_RANK, MLA_HEADS, QK_NOPE_DIM + V_HEAD_DIM)
        w_knt = w_kv[:, :, :QK_NOPE_DIM].reshape(KV_LORA_RANK, -1).T
        w_vt = w_kv[:, :, QK_NOPE_DIM:].reshape(KV_LORA_RANK, -1).T
        ab, qt, k, vt = _mix_in(
            x2d, mod, mix_norm[l], pos_rows, inv_freq, w_in_t,
            q_norm[l], kv_norm[l], w_qt, w_knt, w_vt, cs,
            batch=batch, seq=seq, q_scale=q_scale)
        f = _seq_dft(ab, wa, tc, ts, wb, batch=batch, seq=seq)
        ot = _attention(qt, k, vt, batch=batch, seq=seq)
        x2d = _merge(x2d, mod, mix_norm[l], f, ot, w_in_t, w_fourier_out[l],
                     w_mla_out[l].astype(BF16), w_out[l], seq=seq)

        x2d = _ffn(x2d, mod, ffn2_norm[l], ffn2_w_gate[l], ffn2_w_up[l], ffn2_w_down[l],
                   final_norm, sub=2, final=(l == depth - 1), seq=seq)
    return x2d.reshape(batch, seq, d)
```

```python
import functools

import numpy as np
import jax
import jax.numpy as jnp
from jax import lax
from jax.experimental import pallas as pl
from jax.experimental.pallas import tpu as pltpu

F32 = jnp.float32
BF16 = jnp.bfloat16

FOURIER_GROUPS = 4
FOURIER_GROUP_DIM = 128
FOURIER_WIDTH = FOURIER_GROUPS * FOURIER_GROUP_DIM
MLA_HEADS = 8
QK_NOPE_DIM = 64
QK_ROPE_DIM = 32
QK_DIM = QK_NOPE_DIM + QK_ROPE_DIM
V_HEAD_DIM = 64
Q_LORA_RANK = 384
KV_LORA_RANK = 256
ROPE_THETA = 10000.0
NORM_EPS = 1e-6
N_ADA = 9

HEAD_PAD = 128
ROPE_HALF = QK_ROPE_DIM // 2
FFT_N1 = 128
ROW_TILE = 512
FFN_TILE = 512
KV_TILE = 256
Q_TILE = 512
ATTN_Q_UNROLL = 2
ATTN_AHEAD = 2
ATTN_LAG = 2
ATTN_MAX_EXCESS = 60.0
FFT_PAIRS = 16
VMEM_LIMIT = 56 * 1024 * 1024


def _params(n_axes):
    return pltpu.CompilerParams(
        dimension_semantics=("parallel",) * n_axes, vmem_limit_bytes=VMEM_LIMIT)


def _const_spec(shape):
    nd = len(shape)
    return pl.BlockSpec(shape, lambda *_: (0,) * nd, pipeline_mode=pl.Buffered(1))


def _rms(x, g):
    ms = jnp.mean(x * x, axis=-1, keepdims=True)
    return x * lax.rsqrt(ms + NORM_EPS) * g


def _norm_mod(x, g, shift, scale):
    return _rms(x, g) * (1.0 + scale) + shift


def _dot(a, b):
    return jnp.dot(a, b, preferred_element_type=F32)


def _dot_nt(a, b):
    return lax.dot_general(a, b, (((1,), (1,)), ((), ())), preferred_element_type=F32)


def _pair_rows(x):
    return pltpu.bitcast(x.astype(BF16), jnp.uint32)


def _dot_tn(a, b):
    return lax.dot_general(a, b, (((0,), (0,)), ((), ())), preferred_element_type=F32)


def _ada_kernel(c_ref, w_ref, b_ref, o_ref):
    c = c_ref[...]
    ca = c * jax.nn.sigmoid(c)
    w = w_ref[...]
    ca_hi = ca.astype(BF16)
    ca_lo = (ca - ca_hi.astype(F32)).astype(BF16)
    w_hi = w.astype(BF16)
    w_lo = (w - w_hi.astype(F32)).astype(BF16)
    o_ref[...] = (_dot(ca_hi, w_hi) + (_dot(ca_hi, w_lo) + _dot(ca_lo, w_hi))) + b_ref[...]


def _ada(c, ada_w, ada_b):
    b, d = c.shape
    rows = 8
    c_pad = jnp.zeros((rows, d), F32).at[:b].set(c)
    out = pl.pallas_call(
        _ada_kernel,
        grid=(N_ADA,),
        in_specs=[pl.BlockSpec((rows, d), lambda j: (0, 0)),
                  pl.BlockSpec((d, d), lambda j: (0, j)),
                  pl.BlockSpec((1, d), lambda j: (0, j))],
        out_specs=pl.BlockSpec((rows, d), lambda j: (0, j)),
        out_shape=jax.ShapeDtypeStruct((rows, N_ADA * d), F32),
        compiler_params=_params(1),
        name="ada",
    )(c_pad, ada_w, ada_b.reshape(1, -1))
    return out[:b].reshape(b, N_ADA, d)


def _ffn_kernel(x_ref, mod_ref, g_ref, wg_ref, wu_ref, wd_ref, fn_ref, o_ref, *, sub, final):
    shift = mod_ref[3 * sub:3 * sub + 1, :]
    scale = mod_ref[3 * sub + 1:3 * sub + 2, :]
    gate = mod_ref[3 * sub + 2:3 * sub + 3, :]
    x = x_ref[...]
    h = _norm_mod(x, g_ref[...], shift, scale)
    gg = _dot(h, wg_ref[...])
    uu = _dot(h, wu_ref[...])
    a = gg * jax.nn.sigmoid(gg) * uu
    y = x + (0.5 * gate) * _dot(a, wd_ref[...])
    if final:
        y = _rms(y, fn_ref[...])
    o_ref[...] = y


def _ffn(x2d, mod, norm_g, wg, wu, wd, final_g, *, sub, final, seq):
    n, d = x2d.shape
    dff = wg.shape[1]
    per_batch = seq // FFN_TILE
    return pl.pallas_call(
        functools.partial(_ffn_kernel, sub=sub, final=final),
        grid=(n // FFN_TILE,),
        in_specs=[pl.BlockSpec((FFN_TILE, d), lambda i: (i, 0)),
                  pl.BlockSpec((None, N_ADA, d), lambda i: (i // per_batch, 0, 0)),
                  _const_spec((1, d)),
                  _const_spec((d, dff)), _const_spec((d, dff)), _const_spec((dff, d)),
                  _const_spec((1, d))],
        out_specs=pl.BlockSpec((FFN_TILE, d), lambda i: (i, 0)),
        out_shape=jax.ShapeDtypeStruct((n, d), F32),
        compiler_params=_params(1),
        name="ffn%d" % sub,
    )(x2d, mod, norm_g.reshape(1, d), wg, wu, wd, final_g.reshape(1, d))


def _mix_in_kernel(x_ref, mod_ref, g_ref, pos_ref, invf_ref, w_in_ref, qn_ref, kvn_ref,
                   w_qt_ref, w_knt_ref, w_vt_ref, cs_ref,
                   ab_ref, qt_ref, k_ref, vt_ref, *, q_scale):
    x = x_ref[...]
    tile = x.shape[0]
    h = _norm_mod(x, g_ref[...], mod_ref[3:4, :], mod_ref[4:5, :])
    n_lat = FOURIER_WIDTH + Q_LORA_RANK + KV_LORA_RANK
    z = _dot_nt(h, w_in_ref[:n_lat, :])

    cs = cs_ref[...]
    for g in range(FOURIER_GROUPS):
        lo = g * FOURIER_GROUP_DIM
        u = z[:, lo:lo + FOURIER_GROUP_DIM].astype(BF16)
        ab = _pair_rows(_dot(u, cs))
        ab_ref[:, lo:lo + FOURIER_GROUP_DIM] = ab[:, :FOURIER_GROUP_DIM]
        ab_ref[:, FOURIER_WIDTH + lo:FOURIER_WIDTH + lo + FOURIER_GROUP_DIM] = ab[:, FOURIER_GROUP_DIM:]

    ang = invf_ref[...] * pos_ref[...]
    cos_t = jnp.cos(ang)
    sin_t = jnp.sin(ang)

    def rope_t(v):
        v1, v2 = v[:ROPE_HALF], v[ROPE_HALF:]
        return v1 * cos_t - v2 * sin_t, v2 * cos_t + v1 * sin_t

    pad = jnp.zeros((HEAD_PAD - QK_DIM, tile), F32)

    q_lat = z[:, FOURIER_WIDTH:FOURIER_WIDTH + Q_LORA_RANK]
    qn = _rms(q_lat, qn_ref[...])
    qt = _dot_nt(w_qt_ref[...], qn)
    for hd in range(MLA_HEADS):
        lo = hd * QK_DIM
        r1, r2 = rope_t(qt[lo + QK_NOPE_DIM:lo + QK_DIM])
        qh = jnp.concatenate([qt[lo:lo + QK_NOPE_DIM], r1, r2], axis=0) * q_scale
        qt_ref[hd] = jnp.concatenate([qh, pad], axis=0).astype(BF16)

    kv_lat = z[:, FOURIER_WIDTH + Q_LORA_RANK:]
    kvn = _rms(kv_lat, kvn_ref[...])
    knt = _dot_nt(w_knt_ref[...], kvn)
    vt = _dot_nt(w_vt_ref[...], kvn)
    kr1, kr2 = rope_t(_dot_nt(w_in_ref[n_lat:, :], h))
    for hd in range(MLA_HEADS):
        lo = hd * QK_NOPE_DIM
        kt = jnp.concatenate([knt[lo:lo + QK_NOPE_DIM], kr1, kr2, pad], axis=0)
        k_ref[:, hd * HEAD_PAD:(hd + 1) * HEAD_PAD] = kt.T.astype(BF16)
        for cch in range(tile // KV_TILE):
            vt_ref[hd, cch] = vt[hd * V_HEAD_DIM:(hd + 1) * V_HEAD_DIM,
                                 cch * KV_TILE:(cch + 1) * KV_TILE].astype(BF16)


def _mix_in(x2d, mod, norm_g, pos_rows, inv_freq, w_in_t, q_norm, kv_norm, w_qt, w_knt, w_vt,
            cs, *, batch, seq, q_scale):
    n, d = x2d.shape
    per_batch = seq // ROW_TILE
    assert ROW_TILE % KV_TILE == 0
    chunks_per_tile = ROW_TILE // KV_TILE
    n_in = FOURIER_WIDTH + Q_LORA_RANK + KV_LORA_RANK + QK_ROPE_DIM
    ispec = [pl.BlockSpec((ROW_TILE, d), lambda i: (i, 0)),
             pl.BlockSpec((None, N_ADA, d), lambda i: (i // per_batch, 0, 0)),
             _const_spec((1, d)),
             pl.BlockSpec((None, 1, ROW_TILE), lambda i: (i, 0, 0)),
             _const_spec(inv_freq.shape),
             pl.BlockSpec((n_in, d), lambda i: (0, 0), pipeline_mode=pl.Buffered(1)),
             _const_spec((1, Q_LORA_RANK)), _const_spec((1, KV_LORA_RANK)),
             _const_spec(w_qt.shape), _const_spec(w_knt.shape), _const_spec(w_vt.shape),
             _const_spec(cs.shape)]
    hp = MLA_HEADS * HEAD_PAD
    ospec = [pl.BlockSpec((ROW_TILE // 2, 2 * FOURIER_WIDTH), lambda i: (i, 0)),
             pl.BlockSpec((None, MLA_HEADS, None, HEAD_PAD, Q_TILE),
                          lambda i: (i // per_batch, 0, i % per_batch, 0, 0)),
             pl.BlockSpec((ROW_TILE, hp), lambda i: (i, 0)),
             pl.BlockSpec((None, MLA_HEADS, chunks_per_tile, V_HEAD_DIM, KV_TILE),
                          lambda i: (i // per_batch, 0, i % per_batch, 0, 0))]
    oshape = [jax.ShapeDtypeStruct((n // 2, 2 * FOURIER_WIDTH), jnp.uint32),
              jax.ShapeDtypeStruct((batch, MLA_HEADS, seq // Q_TILE, HEAD_PAD, Q_TILE), BF16),
              jax.ShapeDtypeStruct((n, hp), BF16),
              jax.ShapeDtypeStruct((batch, MLA_HEADS, seq // KV_TILE, V_HEAD_DIM, KV_TILE), BF16)]
    return pl.pallas_call(
        functools.partial(_mix_in_kernel, q_scale=q_scale),
        grid=(n // ROW_TILE,),
        in_specs=ispec, out_specs=ospec, out_shape=oshape,
        compiler_params=_params(1),
        name="mix_in",
    )(x2d, mod, norm_g.reshape(1, d), pos_rows, inv_freq, w_in_t,
      q_norm.reshape(1, -1), kv_norm.reshape(1, -1), w_qt, w_knt, w_vt, cs)


def _paired_dft(words, w):
    x = pltpu.bitcast(words, BF16)
    k = x.shape[0]
    return _dot(w[:, :k], x[:, :FOURIER_WIDTH]) + _dot(w[:, k:], x[:, FOURIER_WIDTH:])


def _fft_a_kernel(x_ref, w_ref, tc_ref, ts_ref, o_ref):
    w = w_ref[...]
    n2 = w.shape[0] // 4
    for j in range(FFT_PAIRS):
        g = _paired_dft(x_ref[:, j, :], w)
        for par in range(2):
            gr = g[2 * par * n2:(2 * par + 1) * n2]
            gi = g[(2 * par + 1) * n2:(2 * par + 2) * n2]
            tc = jnp.concatenate([tc_ref[2 * j + par]] * FOURIER_GROUPS, axis=1)
            ts = jnp.concatenate([ts_ref[2 * j + par]] * FOURIER_GROUPS, axis=1)
            h = jnp.concatenate([gr * tc - gi * ts, gi * tc + gr * ts], axis=1)
            o_ref[2 * j + par] = _pair_rows(h)


def _fft_b_kernel(x_ref, w_ref, o_ref):
    w = w_ref[...]
    n1 = w.shape[0] // 2
    for j in range(FFT_PAIRS):
        f = _paired_dft(x_ref[:, j, :], w)
        o_ref[2 * j] = f[:n1]
        o_ref[2 * j + 1] = f[n1:]


def _seq_dft(ab_words, wa, tc, ts, wb, *, batch, seq):
    n1, n2 = FFT_N1, seq // FFT_N1
    wd = 2 * FOURIER_WIDTH
    x = ab_words.reshape(batch, n2, n1 // 2, wd)
    rows = 2 * FFT_PAIRS
    y = pl.pallas_call(
        _fft_a_kernel,
        grid=(batch, n1 // rows),
        in_specs=[pl.BlockSpec((None, n2, FFT_PAIRS, wd), lambda b, u: (b, 0, u, 0)),
                  _const_spec(wa.shape),
                  pl.BlockSpec((rows, n2, 128), lambda b, u: (u, 0, 0)),
                  pl.BlockSpec((rows, n2, 128), lambda b, u: (u, 0, 0))],
        out_specs=pl.BlockSpec((None, rows, n2 // 2, wd), lambda b, u: (b, u, 0, 0)),
        out_shape=jax.ShapeDtypeStruct((batch, n1, n2 // 2, wd), jnp.uint32),
        compiler_params=_params(2),
        name="fft_a",
    )(x, wa, tc, ts)
    return pl.pallas_call(
        _fft_b_kernel,
        grid=(batch, n2 // rows),
        in_specs=[pl.BlockSpec((None, n1, FFT_PAIRS, wd), lambda b, v: (b, 0, v, 0)),
                  _const_spec(wb.shape)],
        out_specs=pl.BlockSpec((None, rows, n1, FOURIER_WIDTH), lambda b, v: (b, v, 0, 0)),
        out_shape=jax.ShapeDtypeStruct((batch, n2, n1, FOURIER_WIDTH), F32),
        compiler_params=_params(2),
        name="fft_b",
    )(y, wb)


def _dft_constants(seq):
    n1, n2 = FFT_N1, seq // FFT_N1
    gd = FOURIER_GROUP_DIM
    i = np.arange(gd)
    ang_c = 2.0 * np.pi * np.outer(i, i) / gd
    cs = np.concatenate([np.cos(ang_c), np.sin(ang_c)], axis=1)
    i2 = np.arange(n2)
    ang_a = 2.0 * np.pi * np.outer(i2, i2) / n2
    ca, sa = np.cos(ang_a), np.sin(ang_a)
    i1 = np.arange(n1)
    ang_t = 2.0 * np.pi * np.outer(i1, i2) / seq
    ang_b = 2.0 * np.pi * np.outer(i1, i1) / n1
    norm = 1.0 / np.sqrt(float(seq) * gd)
    cb, sb = np.cos(ang_b) * norm, np.sin(ang_b) * norm

    def spread(re_part, im_part):
        m, r = re_part.shape
        out = np.zeros((2, m, 2, 2 * r))
        for p in range(2):
            out[p, :, 0, p::2] = re_part
            out[p, :, 1, p::2] = im_part
        return out.reshape(2 * m, 4 * r)

    wa = spread(np.concatenate([ca, sa], axis=0), np.concatenate([-sa, ca], axis=0))
    wb = spread(cb, -sb)
    f32 = lambda a: jnp.asarray(np.asarray(a, np.float32))
    tc = jnp.broadcast_to(f32(np.cos(ang_t))[:, :, None], (n1, n2, 128))
    ts = jnp.broadcast_to(f32(np.sin(ang_t))[:, :, None], (n1, n2, 128))
    return f32(cs).astype(BF16), f32(wa).astype(BF16), tc, ts, f32(wb).astype(BF16)


def _attn_kernel(qt_ref, k_ref, vt_ref, o_ref):
    n_q, _, tq = qt_ref.shape
    n_chunks = vt_ref.shape[0]
    ones = jnp.ones((16, KV_TILE), BF16)

    def pv(j, p):
        return _dot(jnp.concatenate([vt_ref[j], ones], axis=0), p)

    def finish(acc):
        return (acc[:V_HEAD_DIM] / acc[V_HEAD_DIM:V_HEAD_DIM + 1]).astype(BF16)

    def streaming_tile(t):
        qt = qt_ref[t]

        def qk(j):
            return _dot(k_ref[j * KV_TILE:(j + 1) * KV_TILE, :], qt)

        s0 = qk(0)
        m = jnp.max(s0, axis=0, keepdims=True)
        inflight = [qk(j) for j in range(1, 1 + ATTN_AHEAD)]
        acc = pv(0, jnp.exp2(s0 - m).astype(BF16))
        excess = jnp.zeros((1, tq), F32)
        pending = []
        for j in range(1, n_chunks):
            s = inflight.pop(0)
            if j + ATTN_AHEAD < n_chunks:
                inflight.append(qk(j + ATTN_AHEAD))
            cm = jnp.max(s, axis=0, keepdims=True)
            acc = acc + pv(j, jnp.exp2(s - m).astype(BF16))
            excess = jnp.maximum(excess, cm - m)
            pending.append(cm)
            if len(pending) >= ATTN_LAG:
                m_new = jnp.maximum(m, pending.pop(0))
                acc = acc * jnp.exp2(m - m_new)
                m = m_new
        o_ref[t] = finish(acc)
        return excess

    def two_pass_tile(t):
        qt = qt_ref[t]

        def body(j, carry):
            m_c, acc_c = carry
            kc = k_ref[pl.ds(pl.multiple_of(j * KV_TILE, KV_TILE), KV_TILE), :]
            s_c = _dot(kc, qt)
            m_n = jnp.maximum(m_c, jnp.max(s_c, axis=0, keepdims=True))
            p_c = jnp.exp2(s_c - m_n).astype(BF16)
            return m_n, jnp.exp2(m_c - m_n) * acc_c + pv(j, p_c)

        m_0 = jnp.full((1, tq), -jnp.inf, F32)
        acc_0 = jnp.zeros((V_HEAD_DIM + 16, tq), F32)
        _, acc_s = lax.fori_loop(0, n_chunks, body, (m_0, acc_0))
        o_ref[t] = finish(acc_s)

    def trip(i, carry):
        tiles = [i * ATTN_Q_UNROLL + u for u in range(ATTN_Q_UNROLL)]
        excess = [streaming_tile(t) for t in tiles]
        for t, e in zip(tiles, excess):
            pl.when(jnp.max(e) > ATTN_MAX_EXCESS)(functools.partial(two_pass_tile, t))
        return carry

    lax.fori_loop(0, n_q // ATTN_Q_UNROLL, trip, 0)


def _attention(qt, k, vt, *, batch, seq):
    n_chunks = seq // KV_TILE
    n_q = seq // Q_TILE
    assert n_q % ATTN_Q_UNROLL == 0
    return pl.pallas_call(
        _attn_kernel,
        grid=(batch, MLA_HEADS),
        in_specs=[pl.BlockSpec((None, None, n_q, HEAD_PAD, Q_TILE), lambda b, h: (b, h, 0, 0, 0)),
                  pl.BlockSpec((None, seq, HEAD_PAD), lambda b, h: (b, 0, h)),
                  pl.BlockSpec((None, None, n_chunks, V_HEAD_DIM, KV_TILE),
                               lambda b, h: (b, h, 0, 0, 0))],
        out_specs=pl.BlockSpec((None, None, n_q, V_HEAD_DIM, Q_TILE), lambda b, h: (b, h, 0, 0, 0)),
        out_shape=jax.ShapeDtypeStruct((batch, MLA_HEADS, n_q, V_HEAD_DIM, Q_TILE), BF16),
        compiler_params=_params(2),
        name="attn",
    )(qt, k.reshape(batch, seq, MLA_HEADS * HEAD_PAD), vt)


def _merge_kernel(x_ref, mod_ref, g_ref, f_ref, ot_ref, wg_ref, wfo_ref, wmo_ref, wout_ref, o_ref):
    x = x_ref[...]
    d = x.shape[1]
    h = _norm_mod(x, g_ref[...], mod_ref[3:4, :], mod_ref[4:5, :])
    gl = _dot_nt(h, wg_ref[wg_ref.shape[0] - 2 * d:, :])
    f = jnp.concatenate([f_ref[:, j, :] for j in range(f_ref.shape[1])], axis=0)
    y_a = _dot(f, wfo_ref[...])
    ot = ot_ref[...]
    y_b = _dot_tn(ot.reshape(ot.shape[0] * ot.shape[1], ot.shape[2]), wmo_ref[...])
    y = jax.nn.sigmoid(gl[:, :d]) * y_a + jax.nn.sigmoid(gl[:, d:]) * y_b
    o_ref[...] = x + mod_ref[5:6, :] * _dot(y, wout_ref[...])


def _merge(x2d, mod, norm_g, f, ot, w_g, w_fo, w_mo, w_out, *, seq):
    n, d = x2d.shape
    per_batch = seq // ROW_TILE
    n2 = f.shape[1]
    assert ROW_TILE % (8 * n2) == 0 and ROW_TILE == Q_TILE
    return pl.pallas_call(
        _merge_kernel,
        grid=(n // ROW_TILE,),
        in_specs=[pl.BlockSpec((ROW_TILE, d), lambda i: (i, 0)),
                  pl.BlockSpec((None, N_ADA, d), lambda i: (i // per_batch, 0, 0)),
                  _const_spec((1, d)),
                  pl.BlockSpec((None, n2, ROW_TILE // n2, FOURIER_WIDTH),
                               lambda i: (i // per_batch, 0, i % per_batch, 0)),
                  pl.BlockSpec((None, MLA_HEADS, None, V_HEAD_DIM, Q_TILE),
                               lambda i: (i // per_batch, 0, i % per_batch, 0, 0)),
                  _const_spec(w_g.shape), _const_spec(w_fo.shape), _const_spec(w_mo.shape),
                  _const_spec(w_out.shape)],
        out_specs=pl.BlockSpec((ROW_TILE, d), lambda i: (i, 0)),
        out_shape=jax.ShapeDtypeStruct((n, d), F32),
        compiler_params=_params(1),
        name="merge",
    )(x2d, mod, norm_g.reshape(1, d), f, ot, w_g, w_fo, w_mo, w_out)


def kernel(x, c, positions, ada_w, ada_b, ffn1_norm, ffn1_w_gate, ffn1_w_up, ffn1_w_down, mix_norm,
           w_in, q_norm, w_q_up, kv_norm, w_kv_up, w_fourier_out, w_mla_out, w_out, ffn2_norm,
           ffn2_w_gate, ffn2_w_up, ffn2_w_down, final_norm):
    batch, seq, d = x.shape
    depth = ada_w.shape[0]
    x2d = x.reshape(batch * seq, d)
    pos_rows = positions.astype(F32).reshape(batch * seq // ROW_TILE, 1, ROW_TILE)
    inv_freq = (1.0 / (ROPE_THETA ** (jnp.arange(ROPE_HALF, dtype=F32) * 2.0 / QK_ROPE_DIM)))
    inv_freq = inv_freq.reshape(ROPE_HALF, 1)
    q_scale = float(QK_DIM ** -0.5 * np.log2(np.e))
    cs, wa, tc, ts, wb = _dft_constants(seq)

    for l in range(depth):
        mod = _ada(c, ada_w[l], ada_b[l])
        x2d = _ffn(x2d, mod, ffn1_norm[l], ffn1_w_gate[l], ffn1_w_up[l], ffn1_w_down[l],
                   final_norm, sub=0, final=False, seq=seq)

        w_in_t = jnp.swapaxes(w_in, 1, 2)[l]
        w_qt = w_q_up[l].T.reshape(MLA_HEADS, QK_DIM, Q_LORA_RANK).reshape(MLA_HEADS * QK_DIM, -1)
        w_kv = w_kv_up[l].reshape(KV_LORA_RANK, MLA_HEADS, QK_NOPE_DIM + V_HEAD_DIM)
        w_knt = w_kv[:, :, :QK_NOPE_DIM].reshape(KV_LORA_RANK, -1).T
        w_vt = w_kv[:, :, QK_NOPE_DIM:].reshape(KV_LORA_RANK, -1).T
        ab, qt, k, vt = _mix_in(
            x2d, mod, mix_norm[l], pos_rows, inv_freq, w_in_t,
            q_norm[l], kv_norm[l], w_qt, w_knt, w_vt, cs,
            batch=batch, seq=seq, q_scale=q_scale)
        f = _seq_dft(ab, wa, tc, ts, wb, batch=batch, seq=seq)
        ot = _attention(qt, k, vt, batch=batch, seq=seq)
        x2d = _merge(x2d, mod, mix_norm[l], f, ot, w_in_t, w_fourier_out[l],
                     w_mla_out[l].astype(BF16), w_out[l], seq=seq)

        x2d = _ffn(x2d, mod, ffn2_norm[l], ffn2_w_gate[l], ffn2_w_up[l], ffn2_w_down[l],
                   final_norm, sub=2, final=(l == depth - 1), seq=seq)
    return x2d.reshape(batch, seq, d)
```

```python
import functools

import numpy as np
import jax
import jax.numpy as jnp
from jax import lax
from jax.experimental import pallas as pl
from jax.experimental.pallas import tpu as pltpu

F32 = jnp.float32
BF16 = jnp.bfloat16

FOURIER_GROUPS = 4
FOURIER_GROUP_DIM = 128
FOURIER_WIDTH = FOURIER_GROUPS * FOURIER_GROUP_DIM
MLA_HEADS = 8
QK_NOPE_DIM = 64
QK_ROPE_DIM = 32
QK_DIM = QK_NOPE_DIM + QK_ROPE_DIM
V_HEAD_DIM = 64
Q_LORA_RANK = 384
KV_LORA_RANK = 256
ROPE_THETA = 10000.0
NORM_EPS = 1e-6
N_ADA = 9

HEAD_PAD = 128
ROPE_HALF = QK_ROPE_DIM // 2
FFT_N1 = 128
ROW_TILE = 512
FFN_TILE = 512
KV_TILE = 256
Q_TILE = 512
ATTN_Q_UNROLL = 2
ATTN_AHEAD = 2
ATTN_LAG = 1
ATTN_MAX_EXCESS = 60.0
FFT_PAIRS = 16
VMEM_LIMIT = 56 * 1024 * 1024


def _params(n_axes):
    return pltpu.CompilerParams(
        dimension_semantics=("parallel",) * n_axes, vmem_limit_bytes=VMEM_LIMIT)


def _const_spec(shape):
    nd = len(shape)
    return pl.BlockSpec(shape, lambda *_: (0,) * nd, pipeline_mode=pl.Buffered(1))


def _rms(x, g):
    ms = jnp.mean(x * x, axis=-1, keepdims=True)
    return x * lax.rsqrt(ms + NORM_EPS) * g


def _norm_mod(x, g, shift, scale):
    return _rms(x, g) * (1.0 + scale) + shift


def _dot(a, b):
    return jnp.dot(a, b, preferred_element_type=F32)


def _dot_nt(a, b):
    return lax.dot_general(a, b, (((1,), (1,)), ((), ())), preferred_element_type=F32)


def _pair_rows(x):
    return pltpu.bitcast(x.astype(BF16), jnp.uint32)


def _dot_tn(a, b):
    return lax.dot_general(a, b, (((0,), (0,)), ((), ())), preferred_element_type=F32)


def _ada_kernel(c_ref, w_ref, b_ref, o_ref):
    c = c_ref[...]
    ca = c * jax.nn.sigmoid(c)
    w = w_ref[...]
    ca_hi = ca.astype(BF16)
    ca_lo = (ca - ca_hi.astype(F32)).astype(BF16)
    w_hi = w.astype(BF16)
    w_lo = (w - w_hi.astype(F32)).astype(BF16)
    o_ref[...] = (_dot(ca_hi, w_hi) + (_dot(ca_hi, w_lo) + _dot(ca_lo, w_hi))) + b_ref[...]


def _ada(c, ada_w, ada_b):
    b, d = c.shape
    rows = 8
    c_pad = jnp.zeros((rows, d), F32).at[:b].set(c)
    out = pl.pallas_call(
        _ada_kernel,
        grid=(N_ADA,),
        in_specs=[pl.BlockSpec((rows, d), lambda j: (0, 0)),
                  pl.BlockSpec((d, d), lambda j: (0, j)),
                  pl.BlockSpec((1, d), lambda j: (0, j))],
        out_specs=pl.BlockSpec((rows, d), lambda j: (0, j)),
        out_shape=jax.ShapeDtypeStruct((rows, N_ADA * d), F32),
        compiler_params=_params(1),
        name="ada",
    )(c_pad, ada_w, ada_b.reshape(1, -1))
    return out[:b].reshape(b, N_ADA, d)


def _ffn_kernel(x_ref, mod_ref, g_ref, wg_ref, wu_ref, wd_ref, fn_ref, o_ref, *, sub, final):
    shift = mod_ref[3 * sub:3 * sub + 1, :]
    scale = mod_ref[3 * sub + 1:3 * sub + 2, :]
    gate = mod_ref[3 * sub + 2:3 * sub + 3, :]
    x = x_ref[...]
    h = _norm_mod(x, g_ref[...], shift, scale)
    gg = _dot(h, wg_ref[...])
    uu = _dot(h, wu_ref[...])
    a = gg * jax.nn.sigmoid(gg) * uu
    y = x + (0.5 * gate) * _dot(a, wd_ref[...])
    if final:
        y = _rms(y, fn_ref[...])
    o_ref[...] = y


def _ffn(x2d, mod, norm_g, wg, wu, wd, final_g, *, sub, final, seq):
    n, d = x2d.shape
    dff = wg.shape[1]
    per_batch = seq // FFN_TILE
    return pl.pallas_call(
        functools.partial(_ffn_kernel, sub=sub, final=final),
        grid=(n // FFN_TILE,),
        in_specs=[pl.BlockSpec((FFN_TILE, d), lambda i: (i, 0)),
                  pl.BlockSpec((None, N_ADA, d), lambda i: (i // per_batch, 0, 0)),
                  _const_spec((1, d)),
                  _const_spec((d, dff)), _const_spec((d, dff)), _const_spec((dff, d)),
                  _const_spec((1, d))],
        out_specs=pl.BlockSpec((FFN_TILE, d), lambda i: (i, 0)),
        out_shape=jax.ShapeDtypeStruct((n, d), F32),
        compiler_params=_params(1),
        name="ffn%d" % sub,
    )(x2d, mod, norm_g.reshape(1, d), wg, wu, wd, final_g.reshape(1, d))


def _mix_in_kernel(x_ref, mod_ref, g_ref, pos_ref, invf_ref, w_in_ref, qn_ref, kvn_ref,
                   w_qt_ref, w_knt_ref, w_vt_ref, cs_ref,
                   ab_ref, qt_ref, k_ref, vt_ref, *, q_scale):
    x = x_ref[...]
    tile = x.shape[0]
    h = _norm_mod(x, g_ref[...], mod_ref[3:4, :], mod_ref[4:5, :])
    n_lat = FOURIER_WIDTH + Q_LORA_RANK + KV_LORA_RANK
    z = _dot_nt(h, w_in_ref[:n_lat, :])

    cs = cs_ref[...]
    for g in range(FOURIER_GROUPS):
        lo = g * FOURIER_GROUP_DIM
        u = z[:, lo:lo + FOURIER_GROUP_DIM].astype(BF16)
        ab = _pair_rows(_dot(u, cs))
        ab_ref[:, lo:lo + FOURIER_GROUP_DIM] = ab[:, :FOURIER_GROUP_DIM]
        ab_ref[:, FOURIER_WIDTH + lo:FOURIER_WIDTH + lo + FOURIER_GROUP_DIM] = ab[:, FOURIER_GROUP_DIM:]

    ang = invf_ref[...] * pos_ref[...]
    cos_t = jnp.cos(ang)
    sin_t = jnp.sin(ang)

    def rope_t(v):
        v1, v2 = v[:ROPE_HALF], v[ROPE_HALF:]
        return v1 * cos_t - v2 * sin_t, v2 * cos_t + v1 * sin_t

    pad = jnp.zeros((HEAD_PAD - QK_DIM, tile), F32)

    q_lat = z[:, FOURIER_WIDTH:FOURIER_WIDTH + Q_LORA_RANK]
    qn = _rms(q_lat, qn_ref[...])
    qt = _dot_nt(w_qt_ref[...], qn)
    for hd in range(MLA_HEADS):
        lo = hd * QK_DIM
        r1, r2 = rope_t(qt[lo + QK_NOPE_DIM:lo + QK_DIM])
        qh = jnp.concatenate([qt[lo:lo + QK_NOPE_DIM], r1, r2], axis=0) * q_scale
        qt_ref[hd] = jnp.concatenate([qh, pad], axis=0).astype(BF16)

    kv_lat = z[:, FOURIER_WIDTH + Q_LORA_RANK:]
    kvn = _rms(kv_lat, kvn_ref[...])
    knt = _dot_nt(w_knt_ref[...], kvn)
    vt = _dot_nt(w_vt_ref[...], kvn)
    kr1, kr2 = rope_t(_dot_nt(w_in_ref[n_lat:, :], h))
    for hd in range(MLA_HEADS):
        lo = hd * QK_NOPE_DIM
        kt = jnp.concatenate([knt[lo:lo + QK_NOPE_DIM], kr1, kr2, pad], axis=0)
        k_ref[:, hd * HEAD_PAD:(hd + 1) * HEAD_PAD] = kt.T.astype(BF16)
        for cch in range(tile // KV_TILE):
            vt_ref[hd, cch] = vt[hd * V_HEAD_DIM:(hd + 1) * V_HEAD_DIM,
                                 cch * KV_TILE:(cch + 1) * KV_TILE].astype(BF16)


def _mix_in(x2d, mod, norm_g, pos_rows, inv_freq, w_in_t, q_norm, kv_norm, w_qt, w_knt, w_vt,
            cs, *, batch, seq, q_scale):
    n, d = x2d.shape
    per_batch = seq // ROW_TILE
    assert ROW_TILE % KV_TILE == 0
    chunks_per_tile = ROW_TILE // KV_TILE
    n_in = FOURIER_WIDTH + Q_LORA_RANK + KV_LORA_RANK + QK_ROPE_DIM
    ispec = [pl.BlockSpec((ROW_TILE, d), lambda i: (i, 0)),
             pl.BlockSpec((None, N_ADA, d), lambda i: (i // per_batch, 0, 0)),
             _const_spec((1, d)),
             pl.BlockSpec((None, 1, ROW_TILE), lambda i: (i, 0, 0)),
             _const_spec(inv_freq.shape),
             pl.BlockSpec((n_in, d), lambda i: (0, 0), pipeline_mode=pl.Buffered(1)),
             _const_spec((1, Q_LORA_RANK)), _const_spec((1, KV_LORA_RANK)),
             _const_spec(w_qt.shape), _const_spec(w_knt.shape), _const_spec(w_vt.shape),
             _const_spec(cs.shape)]
    hp = MLA_HEADS * HEAD_PAD
    ospec = [pl.BlockSpec((ROW_TILE // 2, 2 * FOURIER_WIDTH), lambda i: (i, 0)),
             pl.BlockSpec((None, MLA_HEADS, None, HEAD_PAD, Q_TILE),
                          lambda i: (i // per_batch, 0, i % per_batch, 0, 0)),
             pl.BlockSpec((ROW_TILE, hp), lambda i: (i, 0)),
             pl.BlockSpec((None, MLA_HEADS, chunks_per_tile, V_HEAD_DIM, KV_TILE),
                          lambda i: (i // per_batch, 0, i % per_batch, 0, 0))]
    oshape = [jax.ShapeDtypeStruct((n // 2, 2 * FOURIER_WIDTH), jnp.uint32),
              jax.ShapeDtypeStruct((batch, MLA_HEADS, seq // Q_TILE, HEAD_PAD, Q_TILE), BF16),
              jax.ShapeDtypeStruct((n, hp), BF16),
              jax.ShapeDtypeStruct((batch, MLA_HEADS, seq // KV_TILE, V_HEAD_DIM, KV_TILE), BF16)]
    return pl.pallas_call(
        functools.partial(_mix_in_kernel, q_scale=q_scale),
        grid=(n // ROW_TILE,),
        in_specs=ispec, out_specs=ospec, out_shape=oshape,
        compiler_params=_params(1),
        name="mix_in",
    )(x2d, mod, norm_g.reshape(1, d), pos_rows, inv_freq, w_in_t,
      q_norm.reshape(1, -1), kv_norm.reshape(1, -1), w_qt, w_knt, w_vt, cs)


def _paired_dft(words, w):
    x = pltpu.bitcast(words, BF16)
    k = x.shape[0]
    return _dot(w[:, :k], x[:, :FOURIER_WIDTH]) + _dot(w[:, k:], x[:, FOURIER_WIDTH:])


def _fft_a_kernel(x_ref, w_ref, tc_ref, ts_ref, o_ref):
    w = w_ref[...]
    n2 = w.shape[0] // 4
    for j in range(FFT_PAIRS):
        g = _paired_dft(x_ref[:, j, :], w)
        for par in range(2):
            gr = g[2 * par * n2:(2 * par + 1) * n2]
            gi = g[(2 * par + 1) * n2:(2 * par + 2) * n2]
            tc = jnp.concatenate([tc_ref[2 * j + par]] * FOURIER_GROUPS, axis=1)
            ts = jnp.concatenate([ts_ref[2 * j + par]] * FOURIER_GROUPS, axis=1)
            h = jnp.concatenate([gr * tc - gi * ts, gi * tc + gr * ts], axis=1)
            o_ref[2 * j + par] = _pair_rows(h)


def _fft_b_kernel(x_ref, w_ref, o_ref):
    w = w_ref[...]
    n1 = w.shape[0] // 2
    for j in range(FFT_PAIRS):
        f = _paired_dft(x_ref[:, j, :], w)
        o_ref[2 * j] = f[:n1]
        o_ref[2 * j + 1] = f[n1:]


def _seq_dft(ab_words, wa, tc, ts, wb, *, batch, seq):
    n1, n2 = FFT_N1, seq // FFT_N1
    wd = 2 * FOURIER_WIDTH
    x = ab_words.reshape(batch, n2, n1 // 2, wd)
    rows = 2 * FFT_PAIRS
    y = pl.pallas_call(
        _fft_a_kernel,
        grid=(batch, n1 // rows),
        in_specs=[pl.BlockSpec((None, n2, FFT_PAIRS, wd), lambda b, u: (b, 0, u, 0)),
                  _const_spec(wa.shape),
                  pl.BlockSpec((rows, n2, 128), lambda b, u: (u, 0, 0)),
                  pl.BlockSpec((rows, n2, 128), lambda b, u: (u, 0, 0))],
        out_specs=pl.BlockSpec((None, rows, n2 // 2, wd), lambda b, u: (b, u, 0, 0)),
        out_shape=jax.ShapeDtypeStruct((batch, n1, n2 // 2, wd), jnp.uint32),
        compiler_params=_params(2),
        name="fft_a",
    )(x, wa, tc, ts)
    return pl.pallas_call(
        _fft_b_kernel,
        grid=(batch, n2 // rows),
        in_specs=[pl.BlockSpec((None, n1, FFT_PAIRS, wd), lambda b, v: (b, 0, v, 0)),
                  _const_spec(wb.shape)],
        out_specs=pl.BlockSpec((None, rows, n1, FOURIER_WIDTH), lambda b, v: (b, v, 0, 0)),
        out_shape=jax.ShapeDtypeStruct((batch, n2, n1, FOURIER_WIDTH), F32),
        compiler_params=_params(2),
        name="fft_b",
    )(y, wb)


def _dft_constants(seq):
    n1, n2 = FFT_N1, seq // FFT_N1
    gd = FOURIER_GROUP_DIM
    i = np.arange(gd)
    ang_c = 2.0 * np.pi * np.outer(i, i) / gd
    cs = np.concatenate([np.cos(ang_c), np.sin(ang_c)], axis=1)
    i2 = np.arange(n2)
    ang_a = 2.0 * np.pi * np.outer(i2, i2) / n2
    ca, sa = np.cos(ang_a), np.sin(ang_a)
    i1 = np.arange(n1)
    ang_t = 2.0 * np.pi * np.outer(i1, i2) / seq
    ang_b = 2.0 * np.pi * np.outer(i1, i1) / n1
    norm = 1.0 / np.sqrt(float(seq) * gd)
    cb, sb = np.cos(ang_b) * norm, np.sin(ang_b) * norm

    def spread(re_part, im_part):
        m, r = re_part.shape
        out = np.zeros((2, m, 2, 2 * r))
        for p in range(2):
            out[p, :, 0, p::2] = re_part
            out[p, :, 1, p::2] = im_part
        return out.reshape(2 * m, 4 * r)

    wa = spread(np.concatenate([ca, sa], axis=0), np.concatenate([-sa, ca], axis=0))
    wb = spread(cb, -sb)
    f32 = lambda a: jnp.asarray(np.asarray(a, np.float32))
    tc = jnp.broadcast_to(f32(np.cos(ang_t))[:, :, None], (n1, n2, 128))
    ts = jnp.broadcast_to(f32(np.sin(ang_t))[:, :, None], (n1, n2, 128))
    return f32(cs).astype(BF16), f32(wa).astype(BF16), tc, ts, f32(wb).astype(BF16)


def _attn_kernel(qt_ref, k_ref, vt_ref, o_ref):
    n_q, _, tq = qt_ref.shape
    n_chunks = vt_ref.shape[0]
    ones = jnp.ones((16, KV_TILE), BF16)

    def pv(j, p):
        return _dot(jnp.concatenate([vt_ref[j], ones], axis=0), p)

    def finish(acc):
        return (acc[:V_HEAD_DIM] / acc[V_HEAD_DIM:V_HEAD_DIM + 1]).astype(BF16)

    def streaming_tile(t):
        qt = qt_ref[t]

        def qk(j):
            return _dot(k_ref[j * KV_TILE:(j + 1) * KV_TILE, :], qt)

        s0 = qk(0)
        m = jnp.max(s0, axis=0, keepdims=True)
        inflight = [qk(j) for j in range(1, 1 + ATTN_AHEAD)]
        acc = pv(0, jnp.exp2(s0 - m).astype(BF16))
        excess = jnp.zeros((1, tq), F32)
        pending = []
        for j in range(1, n_chunks):
            s = inflight.pop(0)
            if j + ATTN_AHEAD < n_chunks:
                inflight.append(qk(j + ATTN_AHEAD))
            cm = jnp.max(s, axis=0, keepdims=True)
            acc = acc + pv(j, jnp.exp2(s - m).astype(BF16))
            excess = jnp.maximum(excess, cm - m)
            pending.append(cm)
            if len(pending) >= ATTN_LAG:
                m_new = jnp.maximum(m, pending.pop(0))
                acc = acc * jnp.exp2(m - m_new)
                m = m_new
        o_ref[t] = finish(acc)
        return excess

    def two_pass_tile(t):
        qt = qt_ref[t]

        def body(j, carry):
            m_c, acc_c = carry
            kc = k_ref[pl.ds(pl.multiple_of(j * KV_TILE, KV_TILE), KV_TILE), :]
            s_c = _dot(kc, qt)
            m_n = jnp.maximum(m_c, jnp.max(s_c, axis=0, keepdims=True))
            p_c = jnp.exp2(s_c - m_n).astype(BF16)
            return m_n, jnp.exp2(m_c - m_n) * acc_c + pv(j, p_c)

        m_0 = jnp.full((1, tq), -jnp.inf, F32)
        acc_0 = jnp.zeros((V_HEAD_DIM + 16, tq), F32)
        _, acc_s = lax.fori_loop(0, n_chunks, body, (m_0, acc_0))
        o_ref[t] = finish(acc_s)

    def trip(i, carry):
        tiles = [i * ATTN_Q_UNROLL + u for u in range(ATTN_Q_UNROLL)]
        excess = [streaming_tile(t) for t in tiles]
        @pl.when(jnp.max(functools.reduce(jnp.maximum, excess)) > ATTN_MAX_EXCESS)
        def _():
            for t in tiles:
                two_pass_tile(t)

        return carry

    lax.fori_loop(0, n_q // ATTN_Q_UNROLL, trip, 0)


def _attention(qt, k, vt, *, batch, seq):
    n_chunks = seq // KV_TILE
    n_q = seq // Q_TILE
    assert n_q % ATTN_Q_UNROLL == 0
    return pl.pallas_call(
        _attn_kernel,
        grid=(batch, MLA_HEADS),
        in_specs=[pl.BlockSpec((None, None, n_q, HEAD_PAD, Q_TILE), lambda b, h: (b, h, 0, 0, 0)),
                  pl.BlockSpec((None, seq, HEAD_PAD), lambda b, h: (b, 0, h)),
                  pl.BlockSpec((None, None, n_chunks, V_HEAD_DIM, KV_TILE),
                               lambda b, h: (b, h, 0, 0, 0))],
        out_specs=pl.BlockSpec((None, None, n_q, V_HEAD_DIM, Q_TILE), lambda b, h: (b, h, 0, 0, 0)),
        out_shape=jax.ShapeDtypeStruct((batch, MLA_HEADS, n_q, V_HEAD_DIM, Q_TILE), BF16),
        compiler_params=_params(2),
        name="attn",
    )(qt, k.reshape(batch, seq, MLA_HEADS * HEAD_PAD), vt)


def _merge_kernel(x_ref, mod_ref, g_ref, f_ref, ot_ref, wg_ref, wfo_ref, wmo_ref, wout_ref, o_ref):
    x = x_ref[...]
    d = x.shape[1]
    h = _norm_mod(x, g_ref[...], mod_ref[3:4, :], mod_ref[4:5, :])
    gl = _dot_nt(h, wg_ref[wg_ref.shape[0] - 2 * d:, :])
    f = jnp.concatenate([f_ref[:, j, :] for j in range(f_ref.shape[1])], axis=0)
    y_a = _dot(f, wfo_ref[...])
    ot = ot_ref[...]
    y_b = _dot_tn(ot.reshape(ot.shape[0] * ot.shape[1], ot.shape[2]), wmo_ref[...])
    y = jax.nn.sigmoid(gl[:, :d]) * y_a + jax.nn.sigmoid(gl[:, d:]) * y_b
    o_ref[...] = x + mod_ref[5:6, :] * _dot(y, wout_ref[...])


def _merge(x2d, mod, norm_g, f, ot, w_g, w_fo, w_mo, w_out, *, seq):
    n, d = x2d.shape
    per_batch = seq // ROW_TILE
    n2 = f.shape[1]
    assert ROW_TILE % (8 * n2) == 0 and ROW_TILE == Q_TILE
    return pl.pallas_call(
        _merge_kernel,
        grid=(n // ROW_TILE,),
        in_specs=[pl.BlockSpec((ROW_TILE, d), lambda i: (i, 0)),
                  pl.BlockSpec((None, N_ADA, d), lambda i: (i // per_batch, 0, 0)),
                  _const_spec((1, d)),
                  pl.BlockSpec((None, n2, ROW_TILE // n2, FOURIER_WIDTH),
                               lambda i: (i // per_batch, 0, i % per_batch, 0)),
                  pl.BlockSpec((None, MLA_HEADS, None, V_HEAD_DIM, Q_TILE),
                               lambda i: (i // per_batch, 0, i % per_batch, 0, 0)),
                  _const_spec(w_g.shape), _const_spec(w_fo.shape), _const_spec(w_mo.shape),
                  _const_spec(w_out.shape)],
        out_specs=pl.BlockSpec((ROW_TILE, d), lambda i: (i, 0)),
        out_shape=jax.ShapeDtypeStruct((n, d), F32),
        compiler_params=_params(1),
        name="merge",
    )(x2d, mod, norm_g.reshape(1, d), f, ot, w_g, w_fo, w_mo, w_out)


def kernel(x, c, positions, ada_w, ada_b, ffn1_norm, ffn1_w_gate, ffn1_w_up, ffn1_w_down, mix_norm,
           w_in, q_norm, w_q_up, kv_norm, w_kv_up, w_fourier_out, w_mla_out, w_out, ffn2_norm,
           ffn2_w_gate, ffn2_w_up, ffn2_w_down, final_norm):
    batch, seq, d = x.shape
    depth = ada_w.shape[0]
    x2d = x.reshape(batch * seq, d)
    pos_rows = positions.astype(F32).reshape(batch * seq // ROW_TILE, 1, ROW_TILE)
    inv_freq = (1.0 / (ROPE_THETA ** (jnp.arange(ROPE_HALF, dtype=F32) * 2.0 / QK_ROPE_DIM)))
    inv_freq = inv_freq.reshape(ROPE_HALF, 1)
    q_scale = float(QK_DIM ** -0.5 * np.log2(np.e))
    cs, wa, tc, ts, wb = _dft_constants(seq)

    for l in range(depth):
        mod = _ada(c, ada_w[l], ada_b[l])
        x2d = _ffn(x2d, mod, ffn1_norm[l], ffn1_w_gate[l], ffn1_w_up[l], ffn1_w_down[l],
                   final_norm, sub=0, final=False, seq=seq)

        w_in_t = jnp.swapaxes(w_in, 1, 2)[l]
        w_qt = w_q_up[l].T.reshape(MLA_HEADS, QK_DIM, Q_LORA_RANK).reshape(MLA_HEADS * QK_DIM, -1)
        w_kv = w_kv_up[l].reshape(KV_LORA_RANK, MLA_HEADS, QK_NOPE_DIM + V_HEAD_DIM)
        w_knt = w_kv[:, :, :QK_NOPE_DIM].reshape(KV_LORA_RANK, -1).T
        w_vt = w_kv[:, :, QK_NOPE_DIM:].reshape(KV_LORA_RANK, -1).T
        ab, qt, k, vt = _mix_in(
            x2d, mod, mix_norm[l], pos_rows, inv_freq, w_in_t,
            q_norm[l], kv_norm[l], w_qt, w_knt, w_vt, cs,
            batch=batch, seq=seq, q_scale=q_scale)
        f = _seq_dft(ab, wa, tc, ts, wb, batch=batch, seq=seq)
        ot = _attention(qt, k, vt, batch=batch, seq=seq)
        x2d = _merge(x2d, mod, mix_norm[l], f, ot, w_in_t, w_fourier_out[l],
                     w_mla_out[l].astype(BF16), w_out[l], seq=seq)

        x2d = _ffn(x2d, mod, ffn2_norm[l], ffn2_w_gate[l], ffn2_w_up[l], ffn2_w_down[l],
                   final_norm, sub=2, final=(l == depth - 1), seq=seq)
    return x2d.reshape(batch, seq, d)
```

```python
import functools

import numpy as np
import jax
import jax.numpy as jnp
from jax import lax
from jax.experimental import pallas as pl
from jax.experimental.pallas import tpu as pltpu

F32 = jnp.float32
BF16 = jnp.bfloat16

FOURIER_GROUPS = 4
FOURIER_GROUP_DIM = 128
FOURIER_WIDTH = FOURIER_GROUPS * FOURIER_GROUP_DIM
MLA_HEADS = 8
QK_NOPE_DIM = 64
QK_ROPE_DIM = 32
QK_DIM = QK_NOPE_DIM + QK_ROPE_DIM
V_HEAD_DIM = 64
Q_LORA_RANK = 384
KV_LORA_RANK = 256
ROPE_THETA = 10000.0
NORM_EPS = 1e-6
N_ADA = 9

HEAD_PAD = 128
ROPE_HALF = QK_ROPE_DIM // 2
FFT_N1 = 128
ROW_TILE = 512
FFN_TILE = 512
KV_TILE = 256
Q_TILE = 512
ATTN_Q_UNROLL = 4
ATTN_AHEAD = 2
ATTN_REF_EVERY = 2
ATTN_MAX_EXCESS = 60.0
FFT_PAIRS = 16
VMEM_LIMIT = 56 * 1024 * 1024


def _params(n_axes):
    return pltpu.CompilerParams(
        dimension_semantics=("parallel",) * n_axes, vmem_limit_bytes=VMEM_LIMIT)


def _const_spec(shape):
    nd = len(shape)
    return pl.BlockSpec(shape, lambda *_: (0,) * nd, pipeline_mode=pl.Buffered(1))


def _rms(x, g):
    ms = jnp.mean(x * x, axis=-1, keepdims=True)
    return x * lax.rsqrt(ms + NORM_EPS) * g


def _norm_mod(x, g, shift, scale):
    return _rms(x, g) * (1.0 + scale) + shift


def _dot(a, b):
    return jnp.dot(a, b, preferred_element_type=F32)


def _dot_nt(a, b):
    return lax.dot_general(a, b, (((1,), (1,)), ((), ())), preferred_element_type=F32)


def _pair_rows(x):
    return pltpu.bitcast(x.astype(BF16), jnp.uint32)


def _dot_tn(a, b):
    return lax.dot_general(a, b, (((0,), (0,)), ((), ())), preferred_element_type=F32)


def _ada_kernel(c_ref, w_ref, b_ref, o_ref):
    c = c_ref[...]
    ca = c * jax.nn.sigmoid(c)
    w = w_ref[...]
    ca_hi = ca.astype(BF16)
    ca_lo = (ca - ca_hi.astype(F32)).astype(BF16)
    w_hi = w.astype(BF16)
    w_lo = (w - w_hi.astype(F32)).astype(BF16)
    o_ref[...] = (_dot(ca_hi, w_hi) + (_dot(ca_hi, w_lo) + _dot(ca_lo, w_hi))) + b_ref[...]


def _ada(c, ada_w, ada_b):
    b, d = c.shape
    rows = 8
    c_pad = jnp.zeros((rows, d), F32).at[:b].set(c)
    out = pl.pallas_call(
        _ada_kernel,
        grid=(N_ADA,),
        in_specs=[pl.BlockSpec((rows, d), lambda j: (0, 0)),
                  pl.BlockSpec((d, d), lambda j: (0, j)),
                  pl.BlockSpec((1, d), lambda j: (0, j))],
        out_specs=pl.BlockSpec((rows, d), lambda j: (0, j)),
        out_shape=jax.ShapeDtypeStruct((rows, N_ADA * d), F32),
        compiler_params=_params(1),
        name="ada",
    )(c_pad, ada_w, ada_b.reshape(1, -1))
    return out[:b].reshape(b, N_ADA, d)


def _ffn_kernel(x_ref, mod_ref, g_ref, wg_ref, wu_ref, wd_ref, fn_ref, o_ref, *, sub, final):
    shift = mod_ref[3 * sub:3 * sub + 1, :]
    scale = mod_ref[3 * sub + 1:3 * sub + 2, :]
    gate = mod_ref[3 * sub + 2:3 * sub + 3, :]
    x = x_ref[...]
    h = _norm_mod(x, g_ref[...], shift, scale)
    gg = _dot(h, wg_ref[...])
    uu = _dot(h, wu_ref[...])
    a = gg * jax.nn.sigmoid(gg) * uu
    y = x + (0.5 * gate) * _dot(a, wd_ref[...])
    if final:
        y = _rms(y, fn_ref[...])
    o_ref[...] = y


def _ffn(x2d, mod, norm_g, wg, wu, wd, final_g, *, sub, final, seq):
    n, d = x2d.shape
    dff = wg.shape[1]
    per_batch = seq // FFN_TILE
    return pl.pallas_call(
        functools.partial(_ffn_kernel, sub=sub, final=final),
        grid=(n // FFN_TILE,),
        in_specs=[pl.BlockSpec((FFN_TILE, d), lambda i: (i, 0)),
                  pl.BlockSpec((None, N_ADA, d), lambda i: (i // per_batch, 0, 0)),
                  _const_spec((1, d)),
                  _const_spec((d, dff)), _const_spec((d, dff)), _const_spec((dff, d)),
                  _const_spec((1, d))],
        out_specs=pl.BlockSpec((FFN_TILE, d), lambda i: (i, 0)),
        out_shape=jax.ShapeDtypeStruct((n, d), F32),
        compiler_params=_params(1),
        name="ffn%d" % sub,
    )(x2d, mod, norm_g.reshape(1, d), wg, wu, wd, final_g.reshape(1, d))


def _mix_in_kernel(x_ref, mod_ref, g_ref, pos_ref, invf_ref, w_in_ref, qn_ref, kvn_ref,
                   w_qt_ref, w_knt_ref, w_vt_ref, cs_ref,
                   ab_ref, qt_ref, k_ref, vt_ref, *, q_scale):
    x = x_ref[...]
    tile = x.shape[0]
    h = _norm_mod(x, g_ref[...], mod_ref[3:4, :], mod_ref[4:5, :])
    n_lat = FOURIER_WIDTH + Q_LORA_RANK + KV_LORA_RANK
    z = _dot_nt(h, w_in_ref[:n_lat, :])

    cs = cs_ref[...]
    for g in range(FOURIER_GROUPS):
        lo = g * FOURIER_GROUP_DIM
        u = z[:, lo:lo + FOURIER_GROUP_DIM].astype(BF16)
        ab = _pair_rows(_dot(u, cs))
        ab_ref[:, lo:lo + FOURIER_GROUP_DIM] = ab[:, :FOURIER_GROUP_DIM]
        ab_ref[:, FOURIER_WIDTH + lo:FOURIER_WIDTH + lo + FOURIER_GROUP_DIM] = ab[:, FOURIER_GROUP_DIM:]

    ang = invf_ref[...] * pos_ref[...]
    cos_t = jnp.cos(ang)
    sin_t = jnp.sin(ang)

    def rope_t(v):
        v1, v2 = v[:ROPE_HALF], v[ROPE_HALF:]
        return v1 * cos_t - v2 * sin_t, v2 * cos_t + v1 * sin_t

    pad = jnp.zeros((HEAD_PAD - QK_DIM, tile), F32)

    q_lat = z[:, FOURIER_WIDTH:FOURIER_WIDTH + Q_LORA_RANK]
    qn = _rms(q_lat, qn_ref[...])
    qt = _dot_nt(w_qt_ref[...], qn)
    for hd in range(MLA_HEADS):
        lo = hd * QK_DIM
        r1, r2 = rope_t(qt[lo + QK_NOPE_DIM:lo + QK_DIM])
        qh = jnp.concatenate([qt[lo:lo + QK_NOPE_DIM], r1, r2], axis=0) * q_scale
        qt_ref[hd] = jnp.concatenate([qh, pad], axis=0).astype(BF16)

    kv_lat = z[:, FOURIER_WIDTH + Q_LORA_RANK:]
    kvn = _rms(kv_lat, kvn_ref[...])
    knt = _dot_nt(w_knt_ref[...], kvn)
    vt = _dot_nt(w_vt_ref[...], kvn)
    kr1, kr2 = rope_t(_dot_nt(w_in_ref[n_lat:, :], h))
    for hd in range(MLA_HEADS):
        lo = hd * QK_NOPE_DIM
        kt = jnp.concatenate([knt[lo:lo + QK_NOPE_DIM], kr1, kr2, pad], axis=0)
        k_ref[:, hd * HEAD_PAD:(hd + 1) * HEAD_PAD] = kt.T.astype(BF16)
        for cch in range(tile // KV_TILE):
            vt_ref[hd, cch] = vt[hd * V_HEAD_DIM:(hd + 1) * V_HEAD_DIM,
                                 cch * KV_TILE:(cch + 1) * KV_TILE].astype(BF16)


def _mix_in(x2d, mod, norm_g, pos_rows, inv_freq, w_in_t, q_norm, kv_norm, w_qt, w_knt, w_vt,
            cs, *, batch, seq, q_scale):
    n, d = x2d.shape
    per_batch = seq // ROW_TILE
    assert ROW_TILE % KV_TILE == 0
    chunks_per_tile = ROW_TILE // KV_TILE
    n_in = FOURIER_WIDTH + Q_LORA_RANK + KV_LORA_RANK + QK_ROPE_DIM
    ispec = [pl.BlockSpec((ROW_TILE, d), lambda i: (i, 0)),
             pl.BlockSpec((None, N_ADA, d), lambda i: (i // per_batch, 0, 0)),
             _const_spec((1, d)),
             pl.BlockSpec((None, 1, ROW_TILE), lambda i: (i, 0, 0)),
             _const_spec(inv_freq.shape),
             pl.BlockSpec((n_in, d), lambda i: (0, 0), pipeline_mode=pl.Buffered(1)),
             _const_spec((1, Q_LORA_RANK)), _const_spec((1, KV_LORA_RANK)),
             _const_spec(w_qt.shape), _const_spec(w_knt.shape), _const_spec(w_vt.shape),
             _const_spec(cs.shape)]
    hp = MLA_HEADS * HEAD_PAD
    ospec = [pl.BlockSpec((ROW_TILE // 2, 2 * FOURIER_WIDTH), lambda i: (i, 0)),
             pl.BlockSpec((None, MLA_HEADS, None, HEAD_PAD, Q_TILE),
                          lambda i: (i // per_batch, 0, i % per_batch, 0, 0)),
             pl.BlockSpec((ROW_TILE, hp), lambda i: (i, 0)),
             pl.BlockSpec((None, MLA_HEADS, chunks_per_tile, V_HEAD_DIM, KV_TILE),
                          lambda i: (i // per_batch, 0, i % per_batch, 0, 0))]
    oshape = [jax.ShapeDtypeStruct((n // 2, 2 * FOURIER_WIDTH), jnp.uint32),
              jax.ShapeDtypeStruct((batch, MLA_HEADS, seq // Q_TILE, HEAD_PAD, Q_TILE), BF16),
              jax.ShapeDtypeStruct((n, hp), BF16),
              jax.ShapeDtypeStruct((batch, MLA_HEADS, seq // KV_TILE, V_HEAD_DIM, KV_TILE), BF16)]
    return pl.pallas_call(
        functools.partial(_mix_in_kernel, q_scale=q_scale),
        grid=(n // ROW_TILE,),
        in_specs=ispec, out_specs=ospec, out_shape=oshape,
        compiler_params=_params(1),
        name="mix_in",
    )(x2d, mod, norm_g.reshape(1, d), pos_rows, inv_freq, w_in_t,
      q_norm.reshape(1, -1), kv_norm.reshape(1, -1), w_qt, w_knt, w_vt, cs)


def _paired_dft(words, w):
    x = pltpu.bitcast(words, BF16)
    k = x.shape[0]
    return _dot(w[:, :k], x[:, :FOURIER_WIDTH]) + _dot(w[:, k:], x[:, FOURIER_WIDTH:])


def _fft_a_kernel(x_ref, w_ref, tc_ref, ts_ref, o_ref):
    w = w_ref[...]
    n2 = w.shape[0] // 4
    for j in range(FFT_PAIRS):
        g = _paired_dft(x_ref[:, j, :], w)
        for par in range(2):
            gr = g[2 * par * n2:(2 * par + 1) * n2]
            gi = g[(2 * par + 1) * n2:(2 * par + 2) * n2]
            tc = jnp.concatenate([tc_ref[2 * j + par]] * FOURIER_GROUPS, axis=1)
            ts = jnp.concatenate([ts_ref[2 * j + par]] * FOURIER_GROUPS, axis=1)
            h = jnp.concatenate([gr * tc - gi * ts, gi * tc + gr * ts], axis=1)
            o_ref[2 * j + par] = _pair_rows(h)


def _fft_b_kernel(x_ref, w_ref, o_ref):
    w = w_ref[...]
    n1 = w.shape[0] // 2
    for j in range(FFT_PAIRS):
        f = _paired_dft(x_ref[:, j, :], w)
        o_ref[2 * j] = f[:n1]
        o_ref[2 * j + 1] = f[n1:]


def _seq_dft(ab_words, wa, tc, ts, wb, *, batch, seq):
    n1, n2 = FFT_N1, seq // FFT_N1
    wd = 2 * FOURIER_WIDTH
    x = ab_words.reshape(batch, n2, n1 // 2, wd)
    rows = 2 * FFT_PAIRS
    y = pl.pallas_call(
        _fft_a_kernel,
        grid=(batch, n1 // rows),
        in_specs=[pl.BlockSpec((None, n2, FFT_PAIRS, wd), lambda b, u: (b, 0, u, 0)),
                  _const_spec(wa.shape),
                  pl.BlockSpec((rows, n2, 128), lambda b, u: (u, 0, 0)),
                  pl.BlockSpec((rows, n2, 128), lambda b, u: (u, 0, 0))],
        out_specs=pl.BlockSpec((None, rows, n2 // 2, wd), lambda b, u: (b, u, 0, 0)),
        out_shape=jax.ShapeDtypeStruct((batch, n1, n2 // 2, wd), jnp.uint32),
        compiler_params=_params(2),
        name="fft_a",
    )(x, wa, tc, ts)
    return pl.pallas_call(
        _fft_b_kernel,
        grid=(batch, n2 // rows),
        in_specs=[pl.BlockSpec((None, n1, FFT_PAIRS, wd), lambda b, v: (b, 0, v, 0)),
                  _const_spec(wb.shape)],
        out_specs=pl.BlockSpec((None, rows, n1, FOURIER_WIDTH), lambda b, v: (b, v, 0, 0)),
        out_shape=jax.ShapeDtypeStruct((batch, n2, n1, FOURIER_WIDTH), F32),
        compiler_params=_params(2),
        name="fft_b",
    )(y, wb)


def _dft_constants(seq):
    n1, n2 = FFT_N1, seq // FFT_N1
    gd = FOURIER_GROUP_DIM
    i = np.arange(gd)
    ang_c = 2.0 * np.pi * np.outer(i, i) / gd
    cs = np.concatenate([np.cos(ang_c), np.sin(ang_c)], axis=1)
    i2 = np.arange(n2)
    ang_a = 2.0 * np.pi * np.outer(i2, i2) / n2
    ca, sa = np.cos(ang_a), np.sin(ang_a)
    i1 = np.arange(n1)
    ang_t = 2.0 * np.pi * np.outer(i1, i2) / seq
    ang_b = 2.0 * np.pi * np.outer(i1, i1) / n1
    norm = 1.0 / np.sqrt(float(seq) * gd)
    cb, sb = np.cos(ang_b) * norm, np.sin(ang_b) * norm

    def spread(re_part, im_part):
        m, r = re_part.shape
        out = np.zeros((2, m, 2, 2 * r))
        for p in range(2):
            out[p, :, 0, p::2] = re_part
            out[p, :, 1, p::2] = im_part
        return out.reshape(2 * m, 4 * r)

    wa = spread(np.concatenate([ca, sa], axis=0), np.concatenate([-sa, ca], axis=0))
    wb = spread(cb, -sb)
    f32 = lambda a: jnp.asarray(np.asarray(a, np.float32))
    tc = jnp.broadcast_to(f32(np.cos(ang_t))[:, :, None], (n1, n2, 128))
    ts = jnp.broadcast_to(f32(np.sin(ang_t))[:, :, None], (n1, n2, 128))
    return f32(cs).astype(BF16), f32(wa).astype(BF16), tc, ts, f32(wb).astype(BF16)


def _attn_kernel(qt_ref, k_ref, vt_ref, o_ref):
    n_q, _, tq = qt_ref.shape
    n_chunks = vt_ref.shape[0]
    ones = jnp.ones((16, KV_TILE), BF16)

    def pv(j, p):
        return _dot(jnp.concatenate([vt_ref[j], ones], axis=0), p)

    def finish(acc):
        return (acc[:V_HEAD_DIM] / acc[V_HEAD_DIM:V_HEAD_DIM + 1]).astype(BF16)

    def streaming_tile(t):
        qt = qt_ref[t]

        def qk(j):
            return _dot(k_ref[j * KV_TILE:(j + 1) * KV_TILE, :], qt)

        s0 = qk(0)
        m = jnp.max(s0, axis=0, keepdims=True)
        inflight = [qk(j) for j in range(1, 1 + ATTN_AHEAD)]
        acc = pv(0, jnp.exp2(s0 - m).astype(BF16))
        excess = jnp.zeros((1, tq), F32)
        pending = []
        for j in range(1, n_chunks):
            s = inflight.pop(0)
            if j + ATTN_AHEAD < n_chunks:
                inflight.append(qk(j + ATTN_AHEAD))
            cm = jnp.max(s, axis=0, keepdims=True)
            acc = acc + pv(j, jnp.exp2(s - m).astype(BF16))
            excess = jnp.maximum(excess, cm - m)
            pending.append(cm)
            if len(pending) == ATTN_REF_EVERY:
                m_new = jnp.maximum(m, functools.reduce(jnp.maximum, pending))
                pending = []
                acc = acc * jnp.exp2(m - m_new)
                m = m_new
        o_ref[t] = finish(acc)
        return excess

    def two_pass_tile(t):
        qt = qt_ref[t]

        def body(j, carry):
            m_c, acc_c = carry
            kc = k_ref[pl.ds(pl.multiple_of(j * KV_TILE, KV_TILE), KV_TILE), :]
            s_c = _dot(kc, qt)
            m_n = jnp.maximum(m_c, jnp.max(s_c, axis=0, keepdims=True))
            p_c = jnp.exp2(s_c - m_n).astype(BF16)
            return m_n, jnp.exp2(m_c - m_n) * acc_c + pv(j, p_c)

        m_0 = jnp.full((1, tq), -jnp.inf, F32)
        acc_0 = jnp.zeros((V_HEAD_DIM + 16, tq), F32)
        _, acc_s = lax.fori_loop(0, n_chunks, body, (m_0, acc_0))
        o_ref[t] = finish(acc_s)

    def trip(i, carry):
        tiles = [i * ATTN_Q_UNROLL + u for u in range(ATTN_Q_UNROLL)]
        excess = [streaming_tile(t) for t in tiles]
        @pl.when(jnp.max(functools.reduce(jnp.maximum, excess)) > ATTN_MAX_EXCESS)
        def _():
            for t in tiles:
                two_pass_tile(t)

        return carry

    lax.fori_loop(0, n_q // ATTN_Q_UNROLL, trip, 0)


def _attention(qt, k, vt, *, batch, seq):
    n_chunks = seq // KV_TILE
    n_q = seq // Q_TILE
    assert n_q % ATTN_Q_UNROLL == 0
    return pl.pallas_call(
        _attn_kernel,
        grid=(batch, MLA_HEADS),
        in_specs=[pl.BlockSpec((None, None, n_q, HEAD_PAD, Q_TILE), lambda b, h: (b, h, 0, 0, 0)),
                  pl.BlockSpec((None, seq, HEAD_PAD), lambda b, h: (b, 0, h)),
                  pl.BlockSpec((None, None, n_chunks, V_HEAD_DIM, KV_TILE),
                               lambda b, h: (b, h, 0, 0, 0))],
        out_specs=pl.BlockSpec((None, None, n_q, V_HEAD_DIM, Q_TILE), lambda b, h: (b, h, 0, 0, 0)),
        out_shape=jax.ShapeDtypeStruct((batch, MLA_HEADS, n_q, V_HEAD_DIM, Q_TILE), BF16),
        compiler_params=_params(2),
        name="attn",
    )(qt, k.reshape(batch, seq, MLA_HEADS * HEAD_PAD), vt)


def _merge_kernel(x_ref, mod_ref, g_ref, f_ref, ot_ref, wg_ref, wfo_ref, wmo_ref, wout_ref, o_ref):
    x = x_ref[...]
    d = x.shape[1]
    h = _norm_mod(x, g_ref[...], mod_ref[3:4, :], mod_ref[4:5, :])
    gl = _dot_nt(h, wg_ref[wg_ref.shape[0] - 2 * d:, :])
    f = jnp.concatenate([f_ref[:, j, :] for j in range(f_ref.shape[1])], axis=0)
    y_a = _dot(f, wfo_ref[...])
    ot = ot_ref[...]
    y_b = _dot_tn(ot.reshape(ot.shape[0] * ot.shape[1], ot.shape[2]), wmo_ref[...])
    y = jax.nn.sigmoid(gl[:, :d]) * y_a + jax.nn.sigmoid(gl[:, d:]) * y_b
    o_ref[...] = x + mod_ref[5:6, :] * _dot(y, wout_ref[...])


def _merge(x2d, mod, norm_g, f, ot, w_g, w_fo, w_mo, w_out, *, seq):
    n, d = x2d.shape
    per_batch = seq // ROW_TILE
    n2 = f.shape[1]
    assert ROW_TILE % (8 * n2) == 0 and ROW_TILE == Q_TILE
    return pl.pallas_call(
        _merge_kernel,
        grid=(n // ROW_TILE,),
        in_specs=[pl.BlockSpec((ROW_TILE, d), lambda i: (i, 0)),
                  pl.BlockSpec((None, N_ADA, d), lambda i: (i // per_batch, 0, 0)),
                  _const_spec((1, d)),
                  pl.BlockSpec((None, n2, ROW_TILE // n2, FOURIER_WIDTH),
                               lambda i: (i // per_batch, 0, i % per_batch, 0)),
                  pl.BlockSpec((None, MLA_HEADS, None, V_HEAD_DIM, Q_TILE),
                               lambda i: (i // per_batch, 0, i % per_batch, 0, 0)),
                  _const_spec(w_g.shape), _const_spec(w_fo.shape), _const_spec(w_mo.shape),
                  _const_spec(w_out.shape)],
        out_specs=pl.BlockSpec((ROW_TILE, d), lambda i: (i, 0)),
        out_shape=jax.ShapeDtypeStruct((n, d), F32),
        compiler_params=_params(1),
        name="merge",
    )(x2d, mod, norm_g.reshape(1, d), f, ot, w_g, w_fo, w_mo, w_out)


def kernel(x, c, positions, ada_w, ada_b, ffn1_norm, ffn1_w_gate, ffn1_w_up, ffn1_w_down, mix_norm,
           w_in, q_norm, w_q_up, kv_norm, w_kv_up, w_fourier_out, w_mla_out, w_out, ffn2_norm,
           ffn2_w_gate, ffn2_w_up, ffn2_w_down, final_norm):
    batch, seq, d = x.shape
    depth = ada_w.shape[0]
    x2d = x.reshape(batch * seq, d)
    pos_rows = positions.astype(F32).reshape(batch * seq // ROW_TILE, 1, ROW_TILE)
    inv_freq = (1.0 / (ROPE_THETA ** (jnp.arange(ROPE_HALF, dtype=F32) * 2.0 / QK_ROPE_DIM)))
    inv_freq = inv_freq.reshape(ROPE_HALF, 1)
    q_scale = float(QK_DIM ** -0.5 * np.log2(np.e))
    cs, wa, tc, ts, wb = _dft_constants(seq)

    for l in range(depth):
        mod = _ada(c, ada_w[l], ada_b[l])
        x2d = _ffn(x2d, mod, ffn1_norm[l], ffn1_w_gate[l], ffn1_w_up[l], ffn1_w_down[l],
                   final_norm, sub=0, final=False, seq=seq)

        w_in_t = jnp.swapaxes(w_in, 1, 2)[l]
        w_qt = w_q_up[l].T.reshape(MLA_HEADS, QK_DIM, Q_LORA_RANK).reshape(MLA_HEADS * QK_DIM, -1)
        w_kv = w_kv_up[l].reshape(KV_LORA_RANK, MLA_HEADS, QK_NOPE_DIM + V_HEAD_DIM)
        w_knt = w_kv[:, :, :QK_NOPE_DIM].reshape(KV_LORA_RANK, -1).T
        w_vt = w_kv[:, :, QK_NOPE_DIM:].reshape(KV_LORA_RANK, -1).T
        ab, qt, k, vt = _mix_in(
            x2d, mod, mix_norm[l], pos_rows, inv_freq, w_in_t,
            q_norm[l], kv_norm[l], w_qt, w_knt, w_vt, cs,
            batch=batch, seq=seq, q_scale=q_scale)
        f = _seq_dft(ab, wa, tc, ts, wb, batch=batch, seq=seq)
        ot = _attention(qt, k, vt, batch=batch, seq=seq)
        x2d = _merge(x2d, mod, mix_norm[l], f, ot, w_in_t, w_fourier_out[l],
                     w_mla_out[l].astype(BF16), w_out[l], seq=seq)

        x2d = _ffn(x2d, mod, ffn2_norm[l], ffn2_w_gate[l], ffn2_w_up[l], ffn2_w_down[l],
                   final_norm, sub=2, final=(l == depth - 1), seq=seq)
    return x2d.reshape(batch, seq, d)
```

```python
import functools

import numpy as np
import jax
import jax.numpy as jnp
from jax import lax
from jax.experimental import pallas as pl
from jax.experimental.pallas import tpu as pltpu

F32 = jnp.float32
BF16 = jnp.bfloat16

FOURIER_GROUPS = 4
FOURIER_GROUP_DIM = 128
FOURIER_WIDTH = FOURIER_GROUPS * FOURIER_GROUP_DIM
MLA_HEADS = 8
QK_NOPE_DIM = 64
QK_ROPE_DIM = 32
QK_DIM = QK_NOPE_DIM + QK_ROPE_DIM
V_HEAD_DIM = 64
Q_LORA_RANK = 384
KV_LORA_RANK = 256
ROPE_THETA = 10000.0
NORM_EPS = 1e-6
N_ADA = 9

HEAD_PAD = 128
ROPE_HALF = QK_ROPE_DIM // 2
FFT_N1 = 128
ROW_TILE = 512
FFN_TILE = 512
KV_TILE = 256
Q_TILE = 512
ATTN_Q_UNROLL = 4
ATTN_AHEAD = 2
ATTN_REF_EVERY = 2
ATTN_MAX_EXCESS = 60.0
FFT_PAIRS = 16
VMEM_LIMIT = 56 * 1024 * 1024


def _params(n_axes):
    return pltpu.CompilerParams(
        dimension_semantics=("parallel",) * n_axes, vmem_limit_bytes=VMEM_LIMIT)


def _const_spec(shape):
    nd = len(shape)
    return pl.BlockSpec(shape, lambda *_: (0,) * nd, pipeline_mode=pl.Buffered(1))


def _rms(x, g):
    ms = jnp.mean(x * x, axis=-1, keepdims=True)
    return x * lax.rsqrt(ms + NORM_EPS) * g


def _norm_mod(x, g, shift, scale):
    return _rms(x, g) * (1.0 + scale) + shift


def _dot(a, b):
    return jnp.dot(a, b, preferred_element_type=F32)


def _dot_nt(a, b):
    return lax.dot_general(a, b, (((1,), (1,)), ((), ())), preferred_element_type=F32)


def _pair_rows(x):
    return pltpu.bitcast(x.astype(BF16), jnp.uint32)


def _dot_tn(a, b):
    return lax.dot_general(a, b, (((0,), (0,)), ((), ())), preferred_element_type=F32)


def _ada_kernel(c_ref, w_ref, b_ref, o_ref):
    c = c_ref[...]
    ca = c * jax.nn.sigmoid(c)
    w = w_ref[...]
    ca_hi = ca.astype(BF16)
    ca_lo = (ca - ca_hi.astype(F32)).astype(BF16)
    w_hi = w.astype(BF16)
    w_lo = (w - w_hi.astype(F32)).astype(BF16)
    o_ref[...] = (_dot(ca_hi, w_hi) + (_dot(ca_hi, w_lo) + _dot(ca_lo, w_hi))) + b_ref[...]


def _ada(c, ada_w, ada_b):
    b, d = c.shape
    rows = 8
    c_pad = jnp.zeros((rows, d), F32).at[:b].set(c)
    out = pl.pallas_call(
        _ada_kernel,
        grid=(N_ADA,),
        in_specs=[pl.BlockSpec((rows, d), lambda j: (0, 0)),
                  pl.BlockSpec((d, d), lambda j: (0, j)),
                  pl.BlockSpec((1, d), lambda j: (0, j))],
        out_specs=pl.BlockSpec((rows, d), lambda j: (0, j)),
        out_shape=jax.ShapeDtypeStruct((rows, N_ADA * d), F32),
        compiler_params=_params(1),
        name="ada",
    )(c_pad, ada_w, ada_b.reshape(1, -1))
    return out[:b].reshape(b, N_ADA, d)


def _ffn_kernel(x_ref, mod_ref, g_ref, wg_ref, wu_ref, wd_ref, fn_ref, o_ref, *, sub, final):
    shift = mod_ref[3 * sub:3 * sub + 1, :]
    scale = mod_ref[3 * sub + 1:3 * sub + 2, :]
    gate = mod_ref[3 * sub + 2:3 * sub + 3, :]
    x = x_ref[...]
    h = _norm_mod(x, g_ref[...], shift, scale)
    gg = _dot(h, wg_ref[...])
    uu = _dot(h, wu_ref[...])
    a = gg * jax.nn.sigmoid(gg) * uu
    y = x + (0.5 * gate) * _dot(a, wd_ref[...])
    if final:
        y = _rms(y, fn_ref[...])
    o_ref[...] = y


def _ffn(x2d, mod, norm_g, wg, wu, wd, final_g, *, sub, final, seq):
    n, d = x2d.shape
    dff = wg.shape[1]
    per_batch = seq // FFN_TILE
    return pl.pallas_call(
        functools.partial(_ffn_kernel, sub=sub, final=final),
        grid=(n // FFN_TILE,),
        in_specs=[pl.BlockSpec((FFN_TILE, d), lambda i: (i, 0)),
                  pl.BlockSpec((None, N_ADA, d), lambda i: (i // per_batch, 0, 0)),
                  _const_spec((1, d)),
                  _const_spec((d, dff)), _const_spec((d, dff)), _const_spec((dff, d)),
                  _const_spec((1, d))],
        out_specs=pl.BlockSpec((FFN_TILE, d), lambda i: (i, 0)),
        out_shape=jax.ShapeDtypeStruct((n, d), F32),
        compiler_params=_params(1),
        name="ffn%d" % sub,
    )(x2d, mod, norm_g.reshape(1, d), wg, wu, wd, final_g.reshape(1, d))


def _mix_in_kernel(x_ref, mod_ref, g_ref, pos_ref, invf_ref, w_in_ref, qn_ref, kvn_ref,
                   w_qt_ref, w_knt_ref, w_vt_ref, cs_ref,
                   ab_ref, qt_ref, k_ref, vt_ref, *, q_scale):
    x = x_ref[...]
    tile = x.shape[0]
    h = _norm_mod(x, g_ref[...], mod_ref[3:4, :], mod_ref[4:5, :])
    n_lat = FOURIER_WIDTH + Q_LORA_RANK + KV_LORA_RANK
    z = _dot_nt(h, w_in_ref[:n_lat, :])

    cs = cs_ref[...]
    for g in range(FOURIER_GROUPS):
        lo = g * FOURIER_GROUP_DIM
        u = z[:, lo:lo + FOURIER_GROUP_DIM].astype(BF16)
        ab = _pair_rows(_dot(u, cs))
        ab_ref[:, lo:lo + FOURIER_GROUP_DIM] = ab[:, :FOURIER_GROUP_DIM]
        ab_ref[:, FOURIER_WIDTH + lo:FOURIER_WIDTH + lo + FOURIER_GROUP_DIM] = ab[:, FOURIER_GROUP_DIM:]

    ang = invf_ref[...] * pos_ref[...]
    cos_t = jnp.cos(ang)
    sin_t = jnp.sin(ang)

    def rope_t(v):
        v1, v2 = v[:ROPE_HALF], v[ROPE_HALF:]
        return v1 * cos_t - v2 * sin_t, v2 * cos_t + v1 * sin_t

    pad = jnp.zeros((HEAD_PAD - QK_DIM, tile), F32)

    q_lat = z[:, FOURIER_WIDTH:FOURIER_WIDTH + Q_LORA_RANK]
    qn = _rms(q_lat, qn_ref[...])
    qt = _dot_nt(w_qt_ref[...], qn)
    for hd in range(MLA_HEADS):
        lo = hd * QK_DIM
        r1, r2 = rope_t(qt[lo + QK_NOPE_DIM:lo + QK_DIM])
        qh = jnp.concatenate([qt[lo:lo + QK_NOPE_DIM], r1, r2], axis=0) * q_scale
        qt_ref[hd] = jnp.concatenate([qh, pad], axis=0).astype(BF16)

    kv_lat = z[:, FOURIER_WIDTH + Q_LORA_RANK:]
    kvn = _rms(kv_lat, kvn_ref[...])
    knt = _dot_nt(w_knt_ref[...], kvn)
    vt = _dot_nt(w_vt_ref[...], kvn)
    kr1, kr2 = rope_t(_dot_nt(w_in_ref[n_lat:, :], h))
    for hd in range(MLA_HEADS):
        lo = hd * QK_NOPE_DIM
        kt = jnp.concatenate([knt[lo:lo + QK_NOPE_DIM], kr1, kr2, pad], axis=0)
        k_ref[:, hd * HEAD_PAD:(hd + 1) * HEAD_PAD] = kt.T.astype(BF16)
        for cch in range(tile // KV_TILE):
            vt_ref[hd, cch] = vt[hd * V_HEAD_DIM:(hd + 1) * V_HEAD_DIM,
                                 cch * KV_TILE:(cch + 1) * KV_TILE].astype(BF16)


def _mix_in(x2d, mod, norm_g, pos_rows, inv_freq, w_in_t, q_norm, kv_norm, w_qt, w_knt, w_vt,
            cs, *, batch, seq, q_scale):
    n, d = x2d.shape
    per_batch = seq // ROW_TILE
    assert ROW_TILE % KV_TILE == 0
    chunks_per_tile = ROW_TILE // KV_TILE
    n_in = FOURIER_WIDTH + Q_LORA_RANK + KV_LORA_RANK + QK_ROPE_DIM
    ispec = [pl.BlockSpec((ROW_TILE, d), lambda i: (i, 0)),
             pl.BlockSpec((None, N_ADA, d), lambda i: (i // per_batch, 0, 0)),
             _const_spec((1, d)),
             pl.BlockSpec((None, 1, ROW_TILE), lambda i: (i, 0, 0)),
             _const_spec(inv_freq.shape),
             pl.BlockSpec((n_in, d), lambda i: (0, 0), pipeline_mode=pl.Buffered(1)),
             _const_spec((1, Q_LORA_RANK)), _const_spec((1, KV_LORA_RANK)),
             _const_spec(w_qt.shape), _const_spec(w_knt.shape), _const_spec(w_vt.shape),
             _const_spec(cs.shape)]
    hp = MLA_HEADS * HEAD_PAD
    ospec = [pl.BlockSpec((ROW_TILE // 2, 2 * FOURIER_WIDTH), lambda i: (i, 0)),
             pl.BlockSpec((None, MLA_HEADS, None, HEAD_PAD, Q_TILE),
                          lambda i: (i // per_batch, 0, i % per_batch, 0, 0)),
             pl.BlockSpec((ROW_TILE, hp), lambda i: (i, 0)),
             pl.BlockSpec((None, MLA_HEADS, chunks_per_tile, V_HEAD_DIM, KV_TILE),
                          lambda i: (i // per_batch, 0, i % per_batch, 0, 0))]
    oshape = [jax.ShapeDtypeStruct((n // 2, 2 * FOURIER_WIDTH), jnp.uint32),
              jax.ShapeDtypeStruct((batch, MLA_HEADS, seq // Q_TILE, HEAD_PAD, Q_TILE), BF16),
              jax.ShapeDtypeStruct((n, hp), BF16),
              jax.ShapeDtypeStruct((batch, MLA_HEADS, seq // KV_TILE, V_HEAD_DIM, KV_TILE), BF16)]
    return pl.pallas_call(
        functools.partial(_mix_in_kernel, q_scale=q_scale),
        grid=(n // ROW_TILE,),
        in_specs=ispec, out_specs=ospec, out_shape=oshape,
        compiler_params=_params(1),
        name="mix_in",
    )(x2d, mod, norm_g.reshape(1, d), pos_rows, inv_freq, w_in_t,
      q_norm.reshape(1, -1), kv_norm.reshape(1, -1), w_qt, w_knt, w_vt, cs)


def _paired_dft(words, w):
    x = pltpu.bitcast(words, BF16)
    k = x.shape[0]
    return _dot(w[:, :k], x[:, :FOURIER_WIDTH]) + _dot(w[:, k:], x[:, FOURIER_WIDTH:])


def _fft_a_kernel(x_ref, w_ref, tc_ref, ts_ref, o_ref):
    w = w_ref[...]
    n2 = w.shape[0] // 4
    for j in range(FFT_PAIRS):
        g = _paired_dft(x_ref[:, j, :], w)
        for par in range(2):
            gr = g[2 * par * n2:(2 * par + 1) * n2]
            gi = g[(2 * par + 1) * n2:(2 * par + 2) * n2]
            tc = jnp.concatenate([tc_ref[2 * j + par]] * FOURIER_GROUPS, axis=1)
            ts = jnp.concatenate([ts_ref[2 * j + par]] * FOURIER_GROUPS, axis=1)
            h = jnp.concatenate([gr * tc - gi * ts, gi * tc + gr * ts], axis=1)
            o_ref[2 * j + par] = _pair_rows(h)


def _fft_b_kernel(x_ref, w_ref, o_ref):
    w = w_ref[...]
    n1 = w.shape[0] // 2
    for j in range(FFT_PAIRS):
        f = _paired_dft(x_ref[:, j, :], w)
        o_ref[2 * j] = f[:n1]
        o_ref[2 * j + 1] = f[n1:]


def _seq_dft(ab_words, wa, tc, ts, wb, *, batch, seq):
    n1, n2 = FFT_N1, seq // FFT_N1
    wd = 2 * FOURIER_WIDTH
    x = ab_words.reshape(batch, n2, n1 // 2, wd)
    rows = 2 * FFT_PAIRS
    def outer_a(x_hbm, w_ref, tc_hbm, ts_hbm, o_hbm):
        def body(x_ref, tc_ref, ts_ref, o_ref):
            _fft_a_kernel(x_ref.at[0], w_ref, tc_ref, ts_ref, o_ref.at[0])

        pltpu.emit_pipeline(
            body,
            grid=(batch, n1 // rows),
            in_specs=[pl.BlockSpec((1, n2, FFT_PAIRS, wd), lambda b, u: (b, 0, u, 0),
                                   pipeline_mode=pl.Buffered(3)),
                      pl.BlockSpec((rows, n2, 128), lambda b, u: (u, 0, 0)),
                      pl.BlockSpec((rows, n2, 128), lambda b, u: (u, 0, 0))],
            out_specs=[pl.BlockSpec((1, rows, n2 // 2, wd), lambda b, u: (b, u, 0, 0))],
        )(x_hbm, tc_hbm, ts_hbm, o_hbm)

    y = pl.pallas_call(
        outer_a,
        in_specs=[pl.BlockSpec(memory_space=pl.ANY), pl.BlockSpec(memory_space=pltpu.VMEM),
                  pl.BlockSpec(memory_space=pl.ANY), pl.BlockSpec(memory_space=pl.ANY)],
        out_specs=pl.BlockSpec(memory_space=pl.ANY),
        out_shape=jax.ShapeDtypeStruct((batch, n1, n2 // 2, wd), jnp.uint32),
        compiler_params=pltpu.CompilerParams(vmem_limit_bytes=VMEM_LIMIT),
        name="fft_a",
    )(x, wa, tc, ts)

    def outer_b(y_hbm, w_ref, o_hbm):
        def body(x_ref, o_ref):
            _fft_b_kernel(x_ref.at[0], w_ref, o_ref.at[0])

        pltpu.emit_pipeline(
            body,
            grid=(batch, n2 // rows),
            in_specs=[pl.BlockSpec((1, n1, FFT_PAIRS, wd), lambda b, v: (b, 0, v, 0),
                                   pipeline_mode=pl.Buffered(3))],
            out_specs=[pl.BlockSpec((1, rows, n1, FOURIER_WIDTH), lambda b, v: (b, v, 0, 0))],
        )(y_hbm, o_hbm)

    return pl.pallas_call(
        outer_b,
        in_specs=[pl.BlockSpec(memory_space=pl.ANY), pl.BlockSpec(memory_space=pltpu.VMEM)],
        out_specs=pl.BlockSpec(memory_space=pl.ANY),
        out_shape=jax.ShapeDtypeStruct((batch, n2, n1, FOURIER_WIDTH), F32),
        compiler_params=pltpu.CompilerParams(vmem_limit_bytes=VMEM_LIMIT),
        name="fft_b",
    )(y, wb)


def _dft_constants(seq):
    n1, n2 = FFT_N1, seq // FFT_N1
    gd = FOURIER_GROUP_DIM
    i = np.arange(gd)
    ang_c = 2.0 * np.pi * np.outer(i, i) / gd
    cs = np.concatenate([np.cos(ang_c), np.sin(ang_c)], axis=1)
    i2 = np.arange(n2)
    ang_a = 2.0 * np.pi * np.outer(i2, i2) / n2
    ca, sa = np.cos(ang_a), np.sin(ang_a)
    i1 = np.arange(n1)
    ang_t = 2.0 * np.pi * np.outer(i1, i2) / seq
    ang_b = 2.0 * np.pi * np.outer(i1, i1) / n1
    norm = 1.0 / np.sqrt(float(seq) * gd)
    cb, sb = np.cos(ang_b) * norm, np.sin(ang_b) * norm

    def spread(re_part, im_part):
        m, r = re_part.shape
        out = np.zeros((2, m, 2, 2 * r))
        for p in range(2):
            out[p, :, 0, p::2] = re_part
            out[p, :, 1, p::2] = im_part
        return out.reshape(2 * m, 4 * r)

    wa = spread(np.concatenate([ca, sa], axis=0), np.concatenate([-sa, ca], axis=0))
    wb = spread(cb, -sb)
    f32 = lambda a: jnp.asarray(np.asarray(a, np.float32))
    tc = jnp.broadcast_to(f32(np.cos(ang_t))[:, :, None], (n1, n2, 128))
    ts = jnp.broadcast_to(f32(np.sin(ang_t))[:, :, None], (n1, n2, 128))
    return f32(cs).astype(BF16), f32(wa).astype(BF16), tc, ts, f32(wb).astype(BF16)


def _attn_kernel(qt_ref, k_ref, vt_ref, o_ref):
    n_q, _, tq = qt_ref.shape
    n_chunks = vt_ref.shape[0]
    ones = jnp.ones((16, KV_TILE), BF16)

    def pv(j, p):
        return _dot(jnp.concatenate([vt_ref[j], ones], axis=0), p)

    def finish(acc):
        return (acc[:V_HEAD_DIM] / acc[V_HEAD_DIM:V_HEAD_DIM + 1]).astype(BF16)

    def streaming_tile(t):
        qt = qt_ref[t]

        def qk(j):
            return _dot(k_ref[j * KV_TILE:(j + 1) * KV_TILE, :], qt)

        s0 = qk(0)
        m = jnp.max(s0, axis=0, keepdims=True)
        inflight = [qk(j) for j in range(1, 1 + ATTN_AHEAD)]
        acc = pv(0, jnp.exp2(s0 - m).astype(BF16))
        excess = jnp.zeros((1, tq), F32)
        pending = []
        for j in range(1, n_chunks):
            s = inflight.pop(0)
            if j + ATTN_AHEAD < n_chunks:
                inflight.append(qk(j + ATTN_AHEAD))
            cm = jnp.max(s, axis=0, keepdims=True)
            acc = acc + pv(j, jnp.exp2(s - m).astype(BF16))
            excess = jnp.maximum(excess, cm - m)
            pending.append(cm)
            if len(pending) == ATTN_REF_EVERY:
                m_new = jnp.maximum(m, functools.reduce(jnp.maximum, pending))
                pending = []
                acc = acc * jnp.exp2(m - m_new)
                m = m_new
        o_ref[t] = finish(acc)
        return excess

    def two_pass_tile(t):
        qt = qt_ref[t]

        def body(j, carry):
            m_c, acc_c = carry
            kc = k_ref[pl.ds(pl.multiple_of(j * KV_TILE, KV_TILE), KV_TILE), :]
            s_c = _dot(kc, qt)
            m_n = jnp.maximum(m_c, jnp.max(s_c, axis=0, keepdims=True))
            p_c = jnp.exp2(s_c - m_n).astype(BF16)
            return m_n, jnp.exp2(m_c - m_n) * acc_c + pv(j, p_c)

        m_0 = jnp.full((1, tq), -jnp.inf, F32)
        acc_0 = jnp.zeros((V_HEAD_DIM + 16, tq), F32)
        _, acc_s = lax.fori_loop(0, n_chunks, body, (m_0, acc_0))
        o_ref[t] = finish(acc_s)

    def trip(i, carry):
        tiles = [i * ATTN_Q_UNROLL + u for u in range(ATTN_Q_UNROLL)]
        excess = [streaming_tile(t) for t in tiles]
        @pl.when(jnp.max(functools.reduce(jnp.maximum, excess)) > ATTN_MAX_EXCESS)
        def _():
            for t in tiles:
                two_pass_tile(t)

        return carry

    lax.fori_loop(0, n_q // ATTN_Q_UNROLL, trip, 0)


def _attention(qt, k, vt, *, batch, seq):
    n_chunks = seq // KV_TILE
    n_q = seq // Q_TILE
    assert n_q % ATTN_Q_UNROLL == 0
    return pl.pallas_call(
        _attn_kernel,
        grid=(batch, MLA_HEADS),
        in_specs=[pl.BlockSpec((None, None, n_q, HEAD_PAD, Q_TILE), lambda b, h: (b, h, 0, 0, 0)),
                  pl.BlockSpec((None, seq, HEAD_PAD), lambda b, h: (b, 0, h)),
                  pl.BlockSpec((None, None, n_chunks, V_HEAD_DIM, KV_TILE),
                               lambda b, h: (b, h, 0, 0, 0))],
        out_specs=pl.BlockSpec((None, None, n_q, V_HEAD_DIM, Q_TILE), lambda b, h: (b, h, 0, 0, 0)),
        out_shape=jax.ShapeDtypeStruct((batch, MLA_HEADS, n_q, V_HEAD_DIM, Q_TILE), BF16),
        compiler_params=_params(2),
        name="attn",
    )(qt, k.reshape(batch, seq, MLA_HEADS * HEAD_PAD), vt)


def _merge_kernel(x_ref, mod_ref, g_ref, f_ref, ot_ref, wg_ref, wfo_ref, wmo_ref, wout_ref, o_ref):
    x = x_ref[...]
    d = x.shape[1]
    h = _norm_mod(x, g_ref[...], mod_ref[3:4, :], mod_ref[4:5, :])
    gl = _dot_nt(h, wg_ref[wg_ref.shape[0] - 2 * d:, :])
    f = jnp.concatenate([f_ref[:, j, :] for j in range(f_ref.shape[1])], axis=0)
    y_a = _dot(f, wfo_ref[...])
    ot = ot_ref[...]
    y_b = _dot_tn(ot.reshape(ot.shape[0] * ot.shape[1], ot.shape[2]), wmo_ref[...])
    y = jax.nn.sigmoid(gl[:, :d]) * y_a + jax.nn.sigmoid(gl[:, d:]) * y_b
    o_ref[...] = x + mod_ref[5:6, :] * _dot(y, wout_ref[...])


def _merge(x2d, mod, norm_g, f, ot, w_g, w_fo, w_mo, w_out, *, seq):
    n, d = x2d.shape
    per_batch = seq // ROW_TILE
    n2 = f.shape[1]
    assert ROW_TILE % (8 * n2) == 0 and ROW_TILE == Q_TILE
    return pl.pallas_call(
        _merge_kernel,
        grid=(n // ROW_TILE,),
        in_specs=[pl.BlockSpec((ROW_TILE, d), lambda i: (i, 0)),
                  pl.BlockSpec((None, N_ADA, d), lambda i: (i // per_batch, 0, 0)),
                  _const_spec((1, d)),
                  pl.BlockSpec((None, n2, ROW_TILE // n2, FOURIER_WIDTH),
                               lambda i: (i // per_batch, 0, i % per_batch, 0)),
                  pl.BlockSpec((None, MLA_HEADS, None, V_HEAD_DIM, Q_TILE),
                               lambda i: (i // per_batch, 0, i % per_batch, 0, 0)),
                  _const_spec(w_g.shape), _const_spec(w_fo.shape), _const_spec(w_mo.shape),
                  _const_spec(w_out.shape)],
        out_specs=pl.BlockSpec((ROW_TILE, d), lambda i: (i, 0)),
        out_shape=jax.ShapeDtypeStruct((n, d), F32),
        compiler_params=_params(1),
        name="merge",
    )(x2d, mod, norm_g.reshape(1, d), f, ot, w_g, w_fo, w_mo, w_out)


def kernel(x, c, positions, ada_w, ada_b, ffn1_norm, ffn1_w_gate, ffn1_w_up, ffn1_w_down, mix_norm,
           w_in, q_norm, w_q_up, kv_norm, w_kv_up, w_fourier_out, w_mla_out, w_out, ffn2_norm,
           ffn2_w_gate, ffn2_w_up, ffn2_w_down, final_norm):
    batch, seq, d = x.shape
    depth = ada_w.shape[0]
    x2d = x.reshape(batch * seq, d)
    pos_rows = positions.astype(F32).reshape(batch * seq // ROW_TILE, 1, ROW_TILE)
    inv_freq = (1.0 / (ROPE_THETA ** (jnp.arange(ROPE_HALF, dtype=F32) * 2.0 / QK_ROPE_DIM)))
    inv_freq = inv_freq.reshape(ROPE_HALF, 1)
    q_scale = float(QK_DIM ** -0.5 * np.log2(np.e))
    cs, wa, tc, ts, wb = _dft_constants(seq)

    for l in range(depth):
        mod = _ada(c, ada_w[l], ada_b[l])
        x2d = _ffn(x2d, mod, ffn1_norm[l], ffn1_w_gate[l], ffn1_w_up[l], ffn1_w_down[l],
                   final_norm, sub=0, final=False, seq=seq)

        w_in_t = jnp.swapaxes(w_in, 1, 2)[l]
        w_qt = w_q_up[l].T.reshape(MLA_HEADS, QK_DIM, Q_LORA_RANK).reshape(MLA_HEADS * QK_DIM, -1)
        w_kv = w_kv_up[l].reshape(KV_LORA_RANK, MLA_HEADS, QK_NOPE_DIM + V_HEAD_DIM)
        w_knt = w_kv[:, :, :QK_NOPE_DIM].reshape(KV_LORA_RANK, -1).T
        w_vt = w_kv[:, :, QK_NOPE_DIM:].reshape(KV_LORA_RANK, -1).T
        ab, qt, k, vt = _mix_in(
            x2d, mod, mix_norm[l], pos_rows, inv_freq, w_in_t,
            q_norm[l], kv_norm[l], w_qt, w_knt, w_vt, cs,
            batch=batch, seq=seq, q_scale=q_scale)
        f = _seq_dft(ab, wa, tc, ts, wb, batch=batch, seq=seq)
        ot = _attention(qt, k, vt, batch=batch, seq=seq)
        x2d = _merge(x2d, mod, mix_norm[l], f, ot, w_in_t, w_fourier_out[l],
                     w_mla_out[l].astype(BF16), w_out[l], seq=seq)

        x2d = _ffn(x2d, mod, ffn2_norm[l], ffn2_w_gate[l], ffn2_w_up[l], ffn2_w_down[l],
                   final_norm, sub=2, final=(l == depth - 1), seq=seq)
    return x2d.reshape(batch, seq, d)
```
